```python
import math, functools
import jax, jax.numpy as jnp
from jax import lax
import numpy as np

D_MODEL = 1024
BATCH = 8
SEQ = 4096
DEPTH = 1
DEC_BATCH = 32
DEC_SEQ = 1
PAST_LEN = 16384
PAGE_SIZE = 128

MIX_WIDTH = D_MODEL
ATT_WIDTH = MIX_WIDTH // 2
SSM_WIDTH = MIX_WIDTH - ATT_WIDTH
HEAD_DIM = 64
N_ATT_HEADS = ATT_WIDTH // HEAD_DIM
SSM_GROUP_CH = 16
N_SSM_GROUPS = SSM_WIDTH // SSM_GROUP_CH
SSM_STATE = 64
IN_WIDTH = 3 * ATT_WIDTH + SSM_WIDTH
D_FF = 2816
CONV_W = 3
Q_BLOCK = 128
RMS_EPS = 1e-6
DT_MIN = 1e-3
DT_MAX = 1e-1
SB_BIAS_INIT = -6.0

kernel_name = "hymba_stickbreak_s5_convffn_step"


def rmsnorm(x, g):
    x32 = x.astype(jnp.float32)
    r = x32 * lax.rsqrt(jnp.mean(x32 * x32, axis=-1, keepdims=True) + RMS_EPS)
    return (r * g.astype(jnp.float32)).astype(x.dtype)


def stick_breaking(q, k, v, bias, q_pos, k_pos):
    z = jnp.einsum('bqhd,bkhd->bhqk', q.astype(jnp.float32), k.astype(jnp.float32)) * (HEAD_DIM ** -0.5)
    z = z + bias.astype(jnp.float32)[None, :, None, None]
    mask = k_pos[None, :] < q_pos[:, None]
    log_keep = jnp.where(mask, jax.nn.log_sigmoid(-z), 0.0)
    later = lax.cumsum(log_keep, axis=3, reverse=True) - log_keep
    w = jnp.where(mask, jnp.exp(jax.nn.log_sigmoid(z) + later), 0.0)
    return jnp.einsum('bhqk,bkhd->bqhd', w, v.astype(jnp.float32)).astype(v.dtype)


def prompt_attention(q, k, v, bias):
    b, t = q.shape[0], q.shape[1]
    nb = t // Q_BLOCK
    pos = jnp.arange(t, dtype=jnp.int32)
    qb = q.reshape(b, nb, Q_BLOCK, N_ATT_HEADS, HEAD_DIM).transpose(1, 0, 2, 3, 4)
    pb = pos.reshape(nb, Q_BLOCK)
    out = lax.map(lambda a: stick_breaking(a[0], k, v, bias, a[1], pos), (qb, pb))
    return out.transpose(1, 0, 2, 3, 4).reshape(b, t, N_ATT_HEADS, HEAD_DIM)


def sample_attention(q, k, v, bias, k_past, v_past):
    n_past, tq = k_past.shape[1], q.shape[1]
    k_all = jnp.concatenate([k_past.astype(k.dtype), k], axis=1)
    v_all = jnp.concatenate([v_past.astype(v.dtype), v], axis=1)
    q_pos = n_past + jnp.arange(tq, dtype=jnp.int32)
    k_pos = jnp.arange(n_past + tq, dtype=jnp.int32)
    return stick_breaking(q, k_all, v_all, bias, q_pos, k_pos)


def s5_ssm(u, h0, lam_re, lam_im, log_dt, b_re, b_im, c_re, c_im, d_skip):
    f32 = jnp.float32
    t = u.shape[1]
    lam = lax.complex(lam_re.astype(f32), lam_im.astype(f32))
    dt = jnp.exp(log_dt.astype(f32))[:, None]
    lam_bar = jnp.exp(lam * dt)
    b_bar = ((lam_bar - 1.0) / lam)[..., None] * lax.complex(b_re.astype(f32), b_im.astype(f32))
    u32 = u.astype(f32)
    bu = jnp.einsum('gpc,btgc->btgp', b_bar, u32)
    a = jnp.broadcast_to(lam_bar[None, None], (1, t) + lam_bar.shape)

    def combine(e1, e2):
        a1, b1 = e1
        a2, b2 = e2
        return a1 * a2, a2 * b1 + b2

    a_cum, h = lax.associative_scan(combine, (a, bu), axis=1)
    h = h + a_cum * h0[:, None]
    c = lax.complex(c_re.astype(f32), c_im.astype(f32))
    y = jnp.einsum('gcp,btgp->btgc', c, h).real + d_skip.astype(f32) * u32
    return y, h[:, -1]


def hybrid_layer(x, attend, h0, conv_prev, p):
    b, t, _ = x.shape
    xn = rmsnorm(x, p['g_pre_mix'])
    proj = xn @ p['w_in']
    q, k, v, u = jnp.split(proj, [ATT_WIDTH, 2 * ATT_WIDTH, 3 * ATT_WIDTH], axis=-1)
    q = q.reshape(b, t, N_ATT_HEADS, HEAD_DIM)
    k = k.reshape(b, t, N_ATT_HEADS, HEAD_DIM)
    v = v.reshape(b, t, N_ATT_HEADS, HEAD_DIM)
    att = attend(q, k, v, p['sb_bias']).reshape(b, t, ATT_WIDTH)
    y_ssm, h_last = s5_ssm(u.reshape(b, t, N_SSM_GROUPS, SSM_GROUP_CH), h0,
                           p['lam_re'], p['lam_im'], p['log_dt'], p['b_re'], p['b_im'],
                           p['c_re'], p['c_im'], p['d_skip'])
    zs = jax.nn.gelu(y_ssm.reshape(b, t, SSM_WIDTH))
    ssm = (zs * jax.nn.sigmoid(zs @ p['w_glu'].astype(jnp.float32))).astype(x.dtype)
    merged = jnp.concatenate([rmsnorm(att, p['g_att_out']), rmsnorm(ssm, p['g_ssm_out'])], axis=-1)
    x = x + rmsnorm(merged @ p['w_out'], p['g_post_mix'])
    xn = rmsnorm(x, p['g_pre_ffn'])
    gate = xn @ p['w_gate']
    up = xn @ p['w_up']
    padded = jnp.concatenate([conv_prev.astype(gate.dtype), gate], axis=1)
    cw = p['conv_w']
    conv = p['conv_b'] + sum(padded[:, i:i + t] * cw[i] for i in range(CONV_W))
    hdn = jax.nn.gelu(conv) * up
    x = x + rmsnorm(hdn @ p['w_down'], p['g_post_ffn'])
    conv_new = padded[:, padded.shape[1] - (CONV_W - 1):]
    return x, k, v, h_last, conv_new


def setup_inputs(seed: int = 0) -> dict:
    key = jax.random.key(seed)
    ks = jax.random.split(key, 32)
    f32 = jnp.float32
    n_pages = PAST_LEN // PAGE_SIZE
    n_used = DEC_BATCH * n_pages
    n_phys = n_used + n_used // 4

    def nrm(k, shape, scale):
        return scale * jax.random.normal(k, shape, f32)

    def gain(k, width):
        return 1.0 + nrm(k, (DEPTH, width), 0.02)

    page_table = jax.random.permutation(ks[7], n_phys)[:n_used].reshape(DEC_BATCH, n_pages).astype(jnp.int32)
    n_idx = jnp.arange(SSM_STATE, dtype=f32)
    gsp = (DEPTH, N_SSM_GROUPS, SSM_STATE)
    return {
        "x_prompt": nrm(ks[0], (BATCH, SEQ, D_MODEL), 1.0),
        "x_sample": nrm(ks[1], (DEC_BATCH, DEC_SEQ, D_MODEL), 1.0),
        "cache_k": nrm(ks[2], (DEPTH, n_phys, PAGE_SIZE, N_ATT_HEADS, HEAD_DIM), 1.0),
        "cache_v": nrm(ks[3], (DEPTH, n_phys, PAGE_SIZE, N_ATT_HEADS, HEAD_DIM), 1.0),
        "state_ssm_re": nrm(ks[4], (DEPTH, DEC_BATCH, N_SSM_GROUPS, SSM_STATE), 0.5),
        "state_ssm_im": nrm(ks[5], (DEPTH, DEC_BATCH, N_SSM_GROUPS, SSM_STATE), 0.5),
        "state_ffn_conv": nrm(ks[6], (DEPTH, DEC_BATCH, CONV_W - 1, D_FF), 1.0),
        "page_table": page_table,
        "g_pre_mix": gain(ks[8], D_MODEL),
        "w_in": nrm(ks[9], (DEPTH, D_MODEL, IN_WIDTH), D_MODEL ** -0.5),
        "sb_bias": SB_BIAS_INIT + nrm(ks[30], (DEPTH, N_ATT_HEADS), 0.1),
        "g_att_out": gain(ks[10], ATT_WIDTH),
        "lam_re": -0.5 + nrm(ks[11], gsp, 0.01),
        "lam_im": math.pi * n_idx + nrm(ks[12], gsp, 0.01),
        "log_dt": jax.random.uniform(ks[13], (DEPTH, N_SSM_GROUPS), f32, math.log(DT_MIN), math.log(DT_MAX)),
        "b_re": nrm(ks[14], (DEPTH, N_SSM_GROUPS, SSM_STATE, SSM_GROUP_CH), (2 * SSM_GROUP_CH) ** -0.5),
        "b_im": nrm(ks[15], (DEPTH, N_SSM_GROUPS, SSM_STATE, SSM_GROUP_CH), (2 * SSM_GROUP_CH) ** -0.5),
        "c_re": nrm(ks[16], (DEPTH, N_SSM_GROUPS, SSM_GROUP_CH, SSM_STATE), SSM_STATE ** -0.5),
        "c_im": nrm(ks[17], (DEPTH, N_SSM_GROUPS, SSM_GROUP_CH, SSM_STATE), SSM_STATE ** -0.5),
        "d_skip": nrm(ks[18], (DEPTH, N_SSM_GROUPS, SSM_GROUP_CH), 1.0),
        "w_glu": nrm(ks[19], (DEPTH, SSM_WIDTH, SSM_WIDTH), SSM_WIDTH ** -0.5),
        "g_ssm_out": gain(ks[20], SSM_WIDTH),
        "w_out": nrm(ks[21], (DEPTH, MIX_WIDTH, D_MODEL), MIX_WIDTH ** -0.5),
        "g_post_mix": gain(ks[22], D_MODEL),
        "g_pre_ffn": gain(ks[23], D_MODEL),
        "w_gate": nrm(ks[24], (DEPTH, D_MODEL, D_FF), D_MODEL ** -0.5),
        "w_up": nrm(ks[25], (DEPTH, D_MODEL, D_FF), D_MODEL ** -0.5),
        "conv_w": nrm(ks[26], (DEPTH, CONV_W, D_FF), CONV_W ** -0.5),
        "conv_b": nrm(ks[27], (DEPTH, D_FF), 0.01),
        "w_down": nrm(ks[28], (DEPTH, D_FF, D_MODEL), D_FF ** -0.5),
        "g_post_ffn": gain(ks[29], D_MODEL),
    }


def reference(x_prompt, x_sample, cache_k, cache_v, state_ssm_re, state_ssm_im, state_ffn_conv,
              page_table, g_pre_mix, w_in, sb_bias, g_att_out, lam_re, lam_im, log_dt, b_re, b_im,
              c_re, c_im, d_skip, w_glu, g_ssm_out, w_out, g_post_mix, g_pre_ffn, w_gate, w_up,
              conv_w, conv_b, w_down, g_post_ffn):
    xp, xs = x_prompt, x_sample
    bp, bs = xp.shape[0], xs.shape[0]
    n_past = page_table.shape[1] * PAGE_SIZE
    kp_l, vp_l, ks_l, vs_l = [], [], [], []
    hrp_l, hip_l, hrs_l, his_l = [], [], [], []
    cp_l, cs_l = [], []
    for l in range(DEPTH):
        p = dict(g_pre_mix=g_pre_mix[l], w_in=w_in[l], sb_bias=sb_bias[l], g_att_out=g_att_out[l],
                 lam_re=lam_re[l], lam_im=lam_im[l], log_dt=log_dt[l], b_re=b_re[l], b_im=b_im[l],
                 c_re=c_re[l], c_im=c_im[l], d_skip=d_skip[l], w_glu=w_glu[l],
                 g_ssm_out=g_ssm_out[l], w_out=w_out[l], g_post_mix=g_post_mix[l],
                 g_pre_ffn=g_pre_ffn[l], w_gate=w_gate[l], w_up=w_up[l], conv_w=conv_w[l],
                 conv_b=conv_b[l], w_down=w_down[l], g_post_ffn=g_post_ffn[l])
        h0p = jnp.zeros((bp, N_SSM_GROUPS, SSM_STATE), jnp.complex64)
        cprev_p = jnp.zeros((bp, CONV_W - 1, D_FF), xp.dtype)
        xp, kp, vp, hp, cp = hybrid_layer(xp, prompt_attention, h0p, cprev_p, p)
        k_past = cache_k[l][page_table].reshape(bs, n_past, N_ATT_HEADS, HEAD_DIM)
        v_past = cache_v[l][page_table].reshape(bs, n_past, N_ATT_HEADS, HEAD_DIM)
        attend_s = functools.partial(sample_attention, k_past=k_past, v_past=v_past)
        h0s = lax.complex(state_ssm_re[l].astype(jnp.float32), state_ssm_im[l].astype(jnp.float32))
        xs, ksn, vsn, hs, cs = hybrid_layer(xs, attend_s, h0s, state_ffn_conv[l], p)
        kp_l.append(kp); vp_l.append(vp); ks_l.append(ksn); vs_l.append(vsn)
        hrp_l.append(hp.real); hip_l.append(hp.imag); hrs_l.append(hs.real); his_l.append(hs.imag)
        cp_l.append(cp); cs_l.append(cs)
    k_prompt = jnp.stack(kp_l)
    v_prompt = jnp.stack(vp_l)
    k_sample = jnp.stack(ks_l)
    v_sample = jnp.stack(vs_l)
    ssm_re_prompt = jnp.stack(hrp_l)
    ssm_im_prompt = jnp.stack(hip_l)
    ssm_re_sample = jnp.stack(hrs_l)
    ssm_im_sample = jnp.stack(his_l)
    conv_prompt = jnp.stack(cp_l)
    conv_sample = jnp.stack(cs_l)
    return (xp, xs, k_prompt, v_prompt, k_sample, v_sample, ssm_re_prompt, ssm_im_prompt,
            ssm_re_sample, ssm_im_sample, conv_prompt, conv_sample)
```

```python
import functools
import math

import jax
import jax.numpy as jnp
from jax import lax
from jax.experimental import pallas as pl
from jax.experimental.pallas import tpu as pltpu

F32 = jnp.float32
BF16 = jnp.bfloat16

RMS_EPS = 1e-6
HEAD_DIM = 64
LANES = 128
SUBLANES = 8
HEADS_PER_BLOCK = LANES // HEAD_DIM
SQRT_2_OVER_PI = math.sqrt(2.0 / math.pi)
V7X_VMEM_BYTES = 64 * 1024 * 1024
VMEM_LIMIT = V7X_VMEM_BYTES - 8 * 1024 * 1024


def _cparams(n_axes):
    return pltpu.CompilerParams(dimension_semantics=("arbitrary",) * n_axes,
                                vmem_limit_bytes=VMEM_LIMIT)


def _rms(x, g):
    return x * lax.rsqrt(jnp.mean(x * x, axis=-1, keepdims=True) + RMS_EPS) * g


def _gelu(x):
    return 0.5 * x * (1.0 + jnp.tanh(SQRT_2_OVER_PI * (x + 0.044715 * (x * x * x))))


def _sigmoid(x):
    return 1.0 / (1.0 + jnp.exp(-x))


def _softplus(z):
    return jnp.maximum(z, 0.0) + jnp.log(1.0 + jnp.exp(-jnp.abs(z)))


def _dot(a, b):
    return jnp.dot(a, b, preferred_element_type=F32)


def _dot_nt(a, b):
    return lax.dot_general(a, b, (((1,), (1,)), ((), ())), preferred_element_type=F32)


def _const_spec(shape):
    return pl.BlockSpec(shape, lambda *_: (0,) * len(shape))


def _inproj_kernel(x_ref, g_ref, w_ref, q_ref, k_ref, v_ref, u_ref):
    xn = _rms(x_ref[...], g_ref[...]).astype(BF16)
    aw = q_ref.shape[-1]
    q_ref[...] = _dot(xn, w_ref[:, 0:aw])
    k_ref[...] = _dot(xn, w_ref[:, aw:2 * aw])
    v_ref[...] = _dot(xn, w_ref[:, 2 * aw:3 * aw])
    u_ref[...] = _dot(xn, w_ref[:, 3 * aw:])


def _inproj(x, g, w_bf, att_w, bm):
    m, d = x.shape
    n = w_bf.shape[1]
    sw = n - 3 * att_w
    row = lambda i: (i, 0)
    return pl.pallas_call(
        _inproj_kernel,
        grid=(m // bm,),
        in_specs=[pl.BlockSpec((bm, d), row), _const_spec((1, d)), _const_spec((d, n))],
        out_specs=[pl.BlockSpec((bm, att_w), row)] * 3 + [pl.BlockSpec((bm, sw), row)],
        out_shape=[jax.ShapeDtypeStruct((m, att_w), F32)] * 3 + [jax.ShapeDtypeStruct((m, sw), F32)],
        compiler_params=_cparams(1),
        name="inproj",
    )(x, g, w_bf)


def _inproj_seq_kernel(x_ref, g_ref, wq_ref, wkt_ref, wvt_ref, wu_ref, q_ref, kt_ref, vt_ref, u_ref):
    xn = _rms(x_ref[...], g_ref[...]).astype(BF16)
    q_ref[...] = _dot(xn, wq_ref[...])
    kt_ref[...] = _dot_nt(wkt_ref[...], xn)
    vt_ref[...] = _dot_nt(wvt_ref[...], xn)
    u_ref[...] = _dot(xn, wu_ref[...])


def _inproj_seq(x, g, w_bf, att_w, b, t, bm):
    m, d = x.shape
    n = w_bf.shape[1]
    sw = n - 3 * att_w
    nb = t // bm
    row = lambda i: (i, 0)
    tspec = pl.BlockSpec((att_w, bm), lambda i: (i // nb, i % nb))
    wq, wk, wv, wu = (w_bf[:, 0:att_w], w_bf[:, att_w:2 * att_w], w_bf[:, 2 * att_w:3 * att_w],
                      w_bf[:, 3 * att_w:])
    return pl.pallas_call(
        _inproj_seq_kernel,
        grid=(m // bm,),
        in_specs=[pl.BlockSpec((bm, d), row), _const_spec((1, d)), _const_spec((d, att_w)),
                  _const_spec((att_w, d)), _const_spec((att_w, d)), _const_spec((d, sw))],
        out_specs=[pl.BlockSpec((bm, att_w), row), tspec, tspec, pl.BlockSpec((bm, sw), row)],
        out_shape=[jax.ShapeDtypeStruct((m, att_w), F32), jax.ShapeDtypeStruct((b * att_w, t), F32),
                   jax.ShapeDtypeStruct((b * att_w, t), F32), jax.ShapeDtypeStruct((m, sw), F32)],
        compiler_params=_cparams(1),
        name="inproj_seq",
    )(x, g, wq, wk.T, wv.T, wu)


def _attn_kernel(bias_ref, q_ref, ktin_ref, vtin_ref, uu_ref, o_ref, kb_ref, vt_ref, *, bq, bk, t):
    hp = pl.program_id(1)
    qi = pl.program_id(2)
    nk = t // bk
    r = bq // bk

    @pl.when(qi == 0)
    def _prep():
        lane = lax.broadcasted_iota(jnp.int32, (bk, LANES), 1)
        for j in range(nk):
            cols = slice(j * bk, (j + 1) * bk)
            kblk = ktin_ref[:, cols].T
            kb_ref[0, j] = jnp.where(lane < HEAD_DIM, kblk, 0.0).astype(BF16)
            kb_ref[1, j] = jnp.where(lane >= HEAD_DIM, kblk, 0.0).astype(BF16)
            vt_ref[j] = vtin_ref[:, cols].astype(BF16)

    qb = (q_ref[...] * (HEAD_DIM ** -0.5)).astype(BF16)
    uu = uu_ref[...]
    row = lax.broadcasted_iota(jnp.int32, (bk, bq), 0)
    col = lax.broadcasted_iota(jnp.int32, (bk, bq), 1)
    outs = []
    for hh in range(HEADS_PER_BLOCK):
        bias = bias_ref[HEADS_PER_BLOCK * hp + hh]

        def tile(j, acc, carry, diag, hh=hh, bias=bias):
            z = _dot_nt(kb_ref[hh, j], qb) + bias
            lk = -_softplus(z)
            if diag is not None:
                vis = (row + diag * bk) < col
                lk = jnp.where(vis, lk, 0.0)
            hi = lk.astype(BF16)
            lo = (lk - hi.astype(F32)).astype(BF16)
            incl = _dot(uu, jnp.concatenate([hi, lo], axis=0))
            w = jnp.exp(z + incl + carry)
            if diag is not None:
                w = jnp.where(vis, w, 0.0)
            vt = vt_ref[j, hh * HEAD_DIM:(hh + 1) * HEAD_DIM, :]
            acc = acc + _dot(vt, w.astype(BF16))
            return acc, carry + incl[0:1, :]

        acc = jnp.zeros((HEAD_DIM, bq), F32)
        carry = jnp.zeros((1, bq), F32)
        for d in reversed(range(r)):
            acc, carry = tile(qi * r + d, acc, carry, d)

        def body(jj, c):
            return tile(qi * r - 1 - jj, c[0], c[1], None)

        acc, carry = lax.fori_loop(0, qi * r, body, (acc, carry))
        outs.append(acc)
    o_ref[...] = jnp.concatenate(outs, axis=0).T


def _suffix_ones(bk):
    s = jnp.arange(bk)[:, None]
    j = jnp.arange(2 * bk)[None, :] % bk
    return (j >= s).astype(BF16)


def _prompt_attention(q, kt, vt, bias, b, t, bq=256, bk=128):
    m, aw = q.shape
    nq = t // bq
    nk = t // bk
    hp = aw // LANES
    qspec = pl.BlockSpec((bq, LANES), lambda bi, h, i: (bi * nq + i, h))
    kvspec = pl.BlockSpec((LANES, t), lambda bi, h, i: (bi * hp + h, 0))
    return pl.pallas_call(
        functools.partial(_attn_kernel, bq=bq, bk=bk, t=t),
        grid=(b, hp, nq),
        in_specs=[pl.BlockSpec(memory_space=pltpu.SMEM), qspec, kvspec, kvspec,
                  _const_spec((bk, 2 * bk))],
        out_specs=qspec,
        out_shape=jax.ShapeDtypeStruct((m, aw), F32),
        scratch_shapes=[pltpu.VMEM((HEADS_PER_BLOCK, nk, bk, LANES), BF16),
                        pltpu.VMEM((nk, LANES, bk), BF16)],
        compiler_params=_cparams(3),
        name="prompt_attention",
    )(bias, q, kt, vt, _suffix_ones(bk))


def _decode_attn_kernel(pt_ref, q_ref, bias_ref, ll_ref, *refs, n_pages_step, n_heads):
    del pt_ref
    g = n_pages_step
    k_refs = refs[:g]
    v_refs = refs[g:2 * g]
    o_ref = refs[2 * g]
    acc_ref, carry_ref = refs[2 * g + 1:]
    s = pl.program_id(1)
    aw = q_ref.shape[-1]

    @pl.when(s == 0)
    def _init():
        acc_ref[...] = jnp.zeros_like(acc_ref)
        carry_ref[...] = jnp.zeros_like(carry_ref)

    head_of_lane = lax.shift_right_logical(lax.broadcasted_iota(jnp.int32, (n_heads, aw), 1), 6)
    own = head_of_lane == lax.broadcasted_iota(jnp.int32, (n_heads, aw), 0)
    qrows = jnp.where(own, q_ref[0] * (HEAD_DIM ** -0.5), 0.0).astype(BF16)
    bias = bias_ref[...]
    ll = ll_ref[...]
    acc = acc_ref[...]
    carry = carry_ref[...]
    page = k_refs[0].shape[-1]
    for i in range(g):
        kt = k_refs[i][0].reshape(aw, page).astype(BF16)
        vt = v_refs[i][0].reshape(aw, page).astype(BF16)
        z = _dot(qrows, kt) + bias
        lk = -_softplus(z)
        hi = lk.astype(BF16)
        lo = (lk - hi.astype(F32)).astype(BF16)
        cs = _dot(jnp.concatenate([hi, lo], axis=1), ll)
        w = jnp.exp(z + cs[:, :page] + carry)
        acc = acc + _dot_nt(w.astype(BF16), vt)
        carry = carry + cs[:, page:]
    acc_ref[...] = acc
    carry_ref[...] = carry

    @pl.when(s == pl.num_programs(1) - 1)
    def _fin():
        o_ref[0] = jnp.sum(jnp.where(own, acc, 0.0), axis=0, keepdims=True)


def _decode_ones(page):
    sp = jnp.arange(2 * page)[:, None] % page
    s = jnp.arange(page)[None, :]
    low = (sp >= s).astype(BF16)
    return jnp.concatenate([low, jnp.ones((2 * page, page), BF16)], axis=1)


def _decode_attention(q, cache_kt, cache_vt, page_table, bias, n_pages_step=8):
    bs, aw = q.shape
    n_phys, n_heads, hd, page = cache_kt.shape
    npg = page_table.shape[1]
    g = n_pages_step
    steps = npg // g

    def page_spec(i):
        return pl.BlockSpec((1, n_heads, hd, page),
                            lambda b, s, pt: (pt[b, npg - 1 - (s * g + i)], 0, 0, 0))

    qspec = pl.BlockSpec((1, 1, aw), lambda b, s, pt: (b, 0, 0))
    grid_spec = pltpu.PrefetchScalarGridSpec(
        num_scalar_prefetch=1,
        grid=(bs, steps),
        in_specs=[qspec,
                  pl.BlockSpec((n_heads, LANES), lambda b, s, pt: (0, 0)),
                  pl.BlockSpec((2 * page, 2 * page), lambda b, s, pt: (0, 0))]
                 + [page_spec(i) for i in range(g)] * 2,
        out_specs=qspec,
        scratch_shapes=[pltpu.VMEM((n_heads, aw), F32), pltpu.VMEM((n_heads, page), F32)],
    )
    bias_b = jnp.broadcast_to(bias[:, None], (n_heads, LANES))
    out = pl.pallas_call(
        functools.partial(_decode_attn_kernel, n_pages_step=g, n_heads=n_heads),
        grid_spec=grid_spec,
        out_shape=jax.ShapeDtypeStruct((bs, 1, aw), F32),
        compiler_params=_cparams(2),
        name="decode_attention",
    )(page_table, q.reshape(bs, 1, aw), bias_b, _decode_ones(page),
      *([cache_kt] * g), *([cache_vt] * g))
    return out.reshape(bs, aw)


def _ssm_prep_kernel(lr_ref, li_ref, ldt_ref, btr_ref, bti_ref, ctr_ref, cti_ref,
                     wbr_ref, wbi_ref, wcr_ref, wci_ref, lpr_ref, lpi_ref, amr_ref, ami_ref,
                     *, log2_cg, log2_p):
    i = pl.program_id(0)
    blk = lr_ref.shape[-1]
    lam_r = lr_ref[...]
    lam_i = li_ref[...]
    dt = jnp.exp(ldt_ref[...])
    mag = jnp.exp(lam_r * dt)
    ang = lam_i * dt
    lb_r = mag * jnp.cos(ang)
    lb_i = mag * jnp.sin(ang)
    nr = lb_r - 1.0
    den = lam_r * lam_r + lam_i * lam_i
    cr = (nr * lam_r + lb_i * lam_i) / den
    ci = (lb_i * lam_r - nr * lam_i) / den

    shp = btr_ref.shape
    rg = lax.shift_right_logical(lax.broadcasted_iota(jnp.int32, shp, 0), log2_cg)
    cg = lax.shift_right_logical(lax.broadcasted_iota(jnp.int32, shp, 1) + i * blk, log2_p)
    same = rg == cg
    btr = btr_ref[...]
    bti = bti_ref[...]
    wbr_ref[...] = jnp.where(same, cr * btr - ci * bti, 0.0).astype(BF16)
    wbi_ref[...] = jnp.where(same, cr * bti + ci * btr, 0.0).astype(BF16)

    shp = ctr_ref.shape
    rg = lax.shift_right_logical(lax.broadcasted_iota(jnp.int32, shp, 0) + i * blk, log2_p)
    cg = lax.shift_right_logical(lax.broadcasted_iota(jnp.int32, shp, 1), log2_cg)
    same = rg == cg
    wcr_ref[...] = jnp.where(same, ctr_ref[...], 0.0).astype(BF16)
    wci_ref[...] = jnp.where(same, -cti_ref[...], 0.0).astype(BF16)

    rows = lax.broadcasted_iota(jnp.int32, (SUBLANES, blk), 0)
    pr, pi = lb_r, lb_i
    lpr = jnp.zeros((SUBLANES, blk), F32)
    lpi = jnp.zeros((SUBLANES, blk), F32)
    slot = 0
    for n in range(1, SUBLANES + 1):
        lpr = jnp.where(rows == n - 1, pr, lpr)
        lpi = jnp.where(rows == n - 1, pi, lpi)
        if n in (1, 2, 4):
            amr_ref[slot] = jnp.where(rows >= n, pr, 0.0)
            ami_ref[slot] = jnp.where(rows >= n, pi, 0.0)
            slot += 1
        pr, pi = pr * lb_r - pi * lb_i, pr * lb_i + pi * lb_r
    lpr_ref[...] = lpr
    lpi_ref[...] = lpi


def _ssm_prep(lam_re, lam_im, log_dt, b_re, b_im, c_re, c_im, nblk=4):
    g, p = lam_re.shape
    cgs = b_re.shape[-1]
    gp, gc = g * p, g * cgs
    blk = gp // nblk
    flat = lambda a: a.reshape(1, gp)
    ldt = flat(jnp.broadcast_to(log_dt[:, None], (g, p)))
    bt = lambda a: jnp.tile(a.transpose(0, 2, 1).reshape(gc, p), (1, g))
    ct = lambda a: jnp.tile(a.transpose(0, 2, 1).reshape(gp, cgs), (1, g))
    lane_blk = lambda shape: pl.BlockSpec(shape, lambda i: (0,) * (len(shape) - 1) + (i,))
    return pl.pallas_call(
        functools.partial(_ssm_prep_kernel, log2_cg=int(math.log2(cgs)), log2_p=int(math.log2(p))),
        grid=(nblk,),
        in_specs=[lane_blk((1, blk))] * 3 + [lane_blk((gc, blk))] * 2
                 + [pl.BlockSpec((blk, gc), lambda i: (i, 0))] * 2,
        out_specs=[lane_blk((gc, blk))] * 2 + [pl.BlockSpec((blk, gc), lambda i: (i, 0))] * 2
                  + [lane_blk((SUBLANES, blk))] * 2 + [lane_blk((3, SUBLANES, blk))] * 2,
        out_shape=[jax.ShapeDtypeStruct((gc, gp), BF16)] * 2 + [jax.ShapeDtypeStruct((gp, gc), BF16)] * 2
                  + [jax.ShapeDtypeStruct((SUBLANES, gp), F32)] * 2
                  + [jax.ShapeDtypeStruct((3, SUBLANES, gp), F32)] * 2,
        compiler_params=_cparams(1),
        name="ssm_prep",
    )(flat(lam_re), flat(lam_im), ldt, bt(b_re), bt(b_im), ct(c_re), ct(c_im))


def _ssm_tail(hr, hi, u, wcr_ref, wci_ref, dsk_ref, wglu_ref):
    y = _dot(hr.astype(BF16), wcr_ref[...]) + _dot(hi.astype(BF16), wci_ref[...]) + dsk_ref[...] * u
    zs = _gelu(y)
    return zs * _sigmoid(_dot(zs.astype(BF16), wglu_ref[...]))


def _ssm_scan_kernel(u_ref, wbr_ref, wbi_ref, wcr_ref, wci_ref, lpr_ref, lpi_ref, amr_ref, ami_ref,
                     dsk_ref, wglu_ref, o_ref, hr_ref, hi_ref, xr_ref, xi_ref, cr_ref, ci_ref,
                     *, lane_blk):
    tc = pl.program_id(1)
    lt = u_ref.shape[0]
    gp = wbr_ref.shape[1]
    ng = lt // SUBLANES

    @pl.when(tc == 0)
    def _init():
        cr_ref[...] = jnp.zeros_like(cr_ref)
        ci_ref[...] = jnp.zeros_like(ci_ref)

    u = u_ref[...]
    ub = u.astype(BF16)
    xr = _dot(ub, wbr_ref[...]).reshape(ng, SUBLANES, gp)
    xi = _dot(ub, wbi_ref[...]).reshape(ng, SUBLANES, gp)
    for idx, d in enumerate((1, 2, 4)):
        ar = amr_ref[idx]
        ai = ami_ref[idx]
        sr = pltpu.roll(xr, d, axis=1)
        si = pltpu.roll(xi, d, axis=1)
        xr, xi = xr + ar * sr - ai * si, xi + ar * si + ai * sr
    xr_ref[...] = xr
    xi_ref[...] = xi

    for lb in range(gp // lane_blk):
        ls = slice(lb * lane_blk, (lb + 1) * lane_blk)
        lpr = lpr_ref[:, ls]
        lpi = lpi_ref[:, ls]

        def body(i, c, ls=ls, lpr=lpr, lpi=lpi):
            hr, hi = c
            r = xr_ref[i, :, ls] + lpr * hr - lpi * hi
            im = xi_ref[i, :, ls] + lpr * hi + lpi * hr
            xr_ref[i, :, ls] = r
            xi_ref[i, :, ls] = im
            last = SUBLANES - 1
            return (jnp.broadcast_to(r[last:, :], r.shape), jnp.broadcast_to(im[last:, :], im.shape))

        hr, hi = lax.fori_loop(0, ng, body, (cr_ref[:, ls], ci_ref[:, ls]))
        cr_ref[:, ls] = hr
        ci_ref[:, ls] = hi

    hr_ref[0] = cr_ref[0:1, :]
    hi_ref[0] = ci_ref[0:1, :]
    o_ref[...] = _ssm_tail(xr_ref[...].reshape(lt, gp), xi_ref[...].reshape(lt, gp), u,
                           wcr_ref, wci_ref, dsk_ref, wglu_ref)


def _ssm_prompt(u, prep, dsk, wglu_bf, b, t, lt=256, lane_blk=1024):
    wbr, wbi, wcr, wci, lpr, lpi, amr, ami = prep
    m, sw = u.shape
    gp = wbr.shape[1]
    nt = t // lt
    row = pl.BlockSpec((lt, sw), lambda bi, i: (bi * nt + i, 0))
    hspec = pl.BlockSpec((1, 1, gp), lambda bi, i: (bi, 0, 0))
    return pl.pallas_call(
        functools.partial(_ssm_scan_kernel, lane_blk=lane_blk),
        grid=(b, nt),
        in_specs=[row, _const_spec((sw, gp)), _const_spec((sw, gp)), _const_spec((gp, sw)),
                  _const_spec((gp, sw)), _const_spec((SUBLANES, gp)), _const_spec((SUBLANES, gp)),
                  _const_spec((3, SUBLANES, gp)), _const_spec((3, SUBLANES, gp)),
                  _const_spec((1, sw)), _const_spec((sw, sw))],
        out_specs=[row, hspec, hspec],
        out_shape=[jax.ShapeDtypeStruct((m, sw), F32), jax.ShapeDtypeStruct((b, 1, gp), F32),
                   jax.ShapeDtypeStruct((b, 1, gp), F32)],
        scratch_shapes=[pltpu.VMEM((lt // SUBLANES, SUBLANES, gp), F32)] * 2
                       + [pltpu.VMEM((SUBLANES, gp), F32)] * 2,
        compiler_params=_cparams(2),
        name="ssm_scan",
    )(u, wbr, wbi, wcr, wci, lpr, lpi, amr, ami, dsk, wglu_bf)


def _ssm_step_kernel(u_ref, h0r_ref, h0i_ref, wbr_ref, wbi_ref, wcr_ref, wci_ref, lpr_ref, lpi_ref,
                     dsk_ref, wglu_ref, o_ref, hr_ref, hi_ref):
    u = u_ref[...]
    ub = u.astype(BF16)
    lr = lpr_ref[0:1, :]
    li = lpi_ref[0:1, :]
    h0r = h0r_ref[...]
    h0i = h0i_ref[...]
    hr = _dot(ub, wbr_ref[...]) + (lr * h0r - li * h0i)
    hi = _dot(ub, wbi_ref[...]) + (lr * h0i + li * h0r)
    hr_ref[...] = hr
    hi_ref[...] = hi
    o_ref[...] = _ssm_tail(hr, hi, u, wcr_ref, wci_ref, dsk_ref, wglu_ref)


def _ssm_step(u, h0r, h0i, prep, dsk, wglu_bf):
    wbr, wbi, wcr, wci, lpr, lpi, _, _ = prep
    m, sw = u.shape
    gp = wbr.shape[1]
    return pl.pallas_call(
        _ssm_step_kernel,
        out_shape=[jax.ShapeDtypeStruct((m, sw), F32), jax.ShapeDtypeStruct((m, gp), F32),
                   jax.ShapeDtypeStruct((m, gp), F32)],
        compiler_params=pltpu.CompilerParams(vmem_limit_bytes=VMEM_LIMIT),
        name="ssm_step",
    )(u, h0r, h0i, wbr, wbi, wcr, wci, lpr, lpi, dsk, wglu_bf)


def _merge_kernel(x_ref, att_ref, ssm_ref, ga_ref, gs_ref, wo_ref, gp_ref, o_ref):
    an = _rms(att_ref[...], ga_ref[...]).astype(BF16)
    sn = _rms(ssm_ref[...], gs_ref[...]).astype(BF16)
    aw = an.shape[-1]
    y = _dot(an, wo_ref[0:aw, :]) + _dot(sn, wo_ref[aw:, :])
    o_ref[...] = x_ref[...] + _rms(y, gp_ref[...])


def _merge(x, att, ssm, ga, gs, wo_bf, gp, bm):
    m, d = x.shape
    aw, sw = att.shape[1], ssm.shape[1]
    row = lambda w: pl.BlockSpec((bm, w), lambda i: (i, 0))
    return pl.pallas_call(
        _merge_kernel,
        grid=(m // bm,),
        in_specs=[row(d), row(aw), row(sw), _const_spec((1, aw)), _const_spec((1, sw)),
                  _const_spec((aw + sw, d)), _const_spec((1, d))],
        out_specs=row(d),
        out_shape=jax.ShapeDtypeStruct((m, d), F32),
        compiler_params=_cparams(1),
        name="merge_outproj",
    )(x, att, ssm, ga, gs, wo_bf, gp)


def _ffn_chunks(f, fc):
    return [slice(c * fc, (c + 1) * fc) for c in range(f // fc)]


def _ffn_seq_kernel(x_ref, g_ref, wg_ref, wu_ref, cw_ref, cb_ref, wd_ref, gpost_ref,
                    o_ref, cn_ref, gs_ref, *, fc, blocks_per_seq):
    i = pl.program_id(0)
    bm = x_ref.shape[0]
    hist = SUBLANES
    first = (i % blocks_per_seq) == 0

    @pl.when(first)
    def _zero_hist():
        gs_ref[0:hist, :] = jnp.zeros((hist, gs_ref.shape[1]), F32)

    @pl.when(jnp.logical_not(first))
    def _keep_hist():
        gs_ref[0:hist, :] = gs_ref[bm:bm + hist, :]

    x = x_ref[...]
    xn = _rms(x, g_ref[...]).astype(BF16)
    acc = jnp.zeros(x.shape, F32)
    for cs in _ffn_chunks(wg_ref.shape[1], fc):
        gate = _dot(xn, wg_ref[:, cs])
        up = _dot(xn, wu_ref[:, cs])
        gs_ref[hist:hist + bm, cs] = gate
        g1 = gs_ref[hist - 1:hist - 1 + bm, cs]
        g2 = gs_ref[hist - 2:hist - 2 + bm, cs]
        conv = cb_ref[:, cs] + g2 * cw_ref[0:1, cs] + g1 * cw_ref[1:2, cs] + gate * cw_ref[2:3, cs]
        acc = acc + _dot((_gelu(conv) * up).astype(BF16), wd_ref[cs, :])
    cn_ref[0] = gs_ref[hist + bm - 2:hist + bm, :]
    o_ref[...] = x + _rms(acc, gpost_ref[...])


def _ffn_weight_specs(d, f):
    return [_const_spec((1, d)), _const_spec((d, f)), _const_spec((d, f)), _const_spec((3, f)),
            _const_spec((1, f)), _const_spec((f, d)), _const_spec((1, d))]


def _ffn_seq(x, g, wg_bf, wu_bf, cw, cb, wd_bf, gpost, b, t, bm=256, fc=1408):
    m, d = x.shape
    f = wg_bf.shape[1]
    bps = t // bm
    row = pl.BlockSpec((bm, d), lambda i: (i, 0))
    return pl.pallas_call(
        functools.partial(_ffn_seq_kernel, fc=fc, blocks_per_seq=bps),
        grid=(m // bm,),
        in_specs=[row] + _ffn_weight_specs(d, f),
        out_specs=[row, pl.BlockSpec((1, 2, f), lambda i: (i // bps, 0, 0))],
        out_shape=[jax.ShapeDtypeStruct((m, d), F32), jax.ShapeDtypeStruct((b, 2, f), F32)],
        scratch_shapes=[pltpu.VMEM((SUBLANES + bm, f), F32)],
        compiler_params=_cparams(1),
        name="convffn_seq",
    )(x, g, wg_bf, wu_bf, cw, cb, wd_bf, gpost)


def _ffn_step_kernel(x_ref, g_ref, wg_ref, wu_ref, cw_ref, cb_ref, wd_ref, gpost_ref, p0_ref, p1_ref,
                     o_ref, gate_ref, *, fc):
    x = x_ref[...]
    xn = _rms(x, g_ref[...]).astype(BF16)
    acc = jnp.zeros(x.shape, F32)
    for cs in _ffn_chunks(wg_ref.shape[1], fc):
        gate = _dot(xn, wg_ref[:, cs])
        up = _dot(xn, wu_ref[:, cs])
        gate_ref[:, cs] = gate
        conv = (cb_ref[:, cs] + p0_ref[:, cs] * cw_ref[0:1, cs] + p1_ref[:, cs] * cw_ref[1:2, cs]
                + gate * cw_ref[2:3, cs])
        acc = acc + _dot((_gelu(conv) * up).astype(BF16), wd_ref[cs, :])
    o_ref[...] = x + _rms(acc, gpost_ref[...])


def _ffn_step(x, g, wg_bf, wu_bf, cw, cb, wd_bf, gpost, p0, p1, fc=1408):
    m, d = x.shape
    f = wg_bf.shape[1]
    return pl.pallas_call(
        functools.partial(_ffn_step_kernel, fc=fc),
        out_shape=[jax.ShapeDtypeStruct((m, d), F32), jax.ShapeDtypeStruct((m, f), F32)],
        compiler_params=pltpu.CompilerParams(vmem_limit_bytes=VMEM_LIMIT),
        name="convffn_step",
    )(x, g, wg_bf, wu_bf, cw, cb, wd_bf, gpost, p0, p1)


def kernel(x_prompt, x_sample, cache_k, cache_v, state_ssm_re, state_ssm_im, state_ffn_conv, page_table,
           g_pre_mix, w_in, sb_bias, g_att_out, lam_re, lam_im, log_dt, b_re, b_im, c_re, c_im, d_skip,
           w_glu, g_ssm_out, w_out, g_post_mix, g_pre_ffn, w_gate, w_up, conv_w, conv_b, w_down,
           g_post_ffn):
    depth = w_in.shape[0]
    assert depth == 1, "single-layer step"
    b, t, d = x_prompt.shape
    bs, ts, _ = x_sample.shape
    assert ts == 1, "sample group advances one token per sequence"
    n_heads, hd = cache_k.shape[-2:]
    assert hd == HEAD_DIM
    aw = n_heads * hd
    g, p = lam_re.shape[1:]
    l = 0

    row = lambda a: a[l].reshape(1, -1)
    w_in_bf = w_in[l].astype(BF16)
    wglu_bf = w_glu[l].astype(BF16)
    wo_bf = w_out[l].astype(BF16)
    wg_bf = w_gate[l].astype(BF16)
    wu_bf = w_up[l].astype(BF16)
    wd_bf = w_down[l].astype(BF16)
    bias = sb_bias[l]
    dsk = row(d_skip)
    prep = _ssm_prep(lam_re[l], lam_im[l], log_dt[l], b_re[l], b_im[l], c_re[l], c_im[l])

    xp = x_prompt.reshape(b * t, d)
    qp, ktp, vtp, up = _inproj_seq(xp, row(g_pre_mix), w_in_bf, aw, b, t, bm=512)
    att_p = _prompt_attention(qp, ktp, vtp, bias, b, t)
    ssm_p, hrp, hip = _ssm_prompt(up, prep, dsk, wglu_bf, b, t)
    xp1 = _merge(xp, att_p, ssm_p, row(g_att_out), row(g_ssm_out), wo_bf, row(g_post_mix), bm=512)
    yp, conv_p = _ffn_seq(xp1, row(g_pre_ffn), wg_bf, wu_bf, conv_w[l], row(conv_b), wd_bf,
                          row(g_post_ffn), b, t)

    xs = x_sample.reshape(bs, d)
    qs, ks, vs, us = _inproj(xs, row(g_pre_mix), w_in_bf, aw, bm=bs)
    att_s = _decode_attention(qs, cache_k[l].transpose(0, 2, 3, 1), cache_v[l].transpose(0, 2, 3, 1),
                              page_table, bias)
    ssm_s, hrs, his = _ssm_step(us, state_ssm_re[l].reshape(bs, g * p), state_ssm_im[l].reshape(bs, g * p),
                                prep, dsk, wglu_bf)
    xs1 = _merge(xs, att_s, ssm_s, row(g_att_out), row(g_ssm_out), wo_bf, row(g_post_mix), bm=bs)
    prev = state_ffn_conv[l]
    ys, gate_s = _ffn_step(xs1, row(g_pre_ffn), wg_bf, wu_bf, conv_w[l], row(conv_b), wd_bf,
                           row(g_post_ffn), prev[:, 0], prev[:, 1])
    conv_s = jnp.stack([prev[:, 1], gate_s], axis=1)

    heads = lambda a, n, s: a.reshape(1, n, s, n_heads, hd)
    heads_t = lambda a: a.reshape(1, b, n_heads, hd, t).transpose(0, 1, 4, 2, 3)
    state = lambda a, n: a.reshape(1, n, g, p)
    return (yp.reshape(b, t, d), ys.reshape(bs, 1, d),
            heads_t(ktp), heads_t(vtp), heads(ks, bs, 1), heads(vs, bs, 1),
            state(hrp, b), state(hip, b), state(hrs, bs), state(his, bs),
            conv_p[None], conv_s[None])
```

```python
import functools
import math

import jax
import jax.numpy as jnp
from jax import lax
from jax.experimental import pallas as pl
from jax.experimental.pallas import tpu as pltpu

F32 = jnp.float32
BF16 = jnp.bfloat16

RMS_EPS = 1e-6
HEAD_DIM = 64
LANES = 128
SUBLANES = 8
HEADS_PER_BLOCK = LANES // HEAD_DIM
SQRT_2_OVER_PI = math.sqrt(2.0 / math.pi)
LOG2E = math.log2(math.e)
SUFFIX_LAG = 3
WEIGH_LAG = 7
V7X_VMEM_BYTES = 64 * 1024 * 1024
VMEM_LIMIT = V7X_VMEM_BYTES - 8 * 1024 * 1024


def _cparams(n_axes):
    return pltpu.CompilerParams(dimension_semantics=("arbitrary",) * n_axes,
                                vmem_limit_bytes=VMEM_LIMIT)


def _rms(x, g):
    return x * lax.rsqrt(jnp.mean(x * x, axis=-1, keepdims=True) + RMS_EPS) * g


def _gelu(x):
    return 0.5 * x * (1.0 + jnp.tanh(SQRT_2_OVER_PI * (x + 0.044715 * (x * x * x))))


def _sigmoid(x):
    return 1.0 / (1.0 + jnp.exp(-x))


def _softplus2(zz):
    return jnp.maximum(zz, 0.0) + jnp.log(1.0 + jnp.exp2(-jnp.abs(zz))) * LOG2E


def _split_bf16(x):
    hi = x.astype(BF16)
    return hi, (x - hi.astype(F32)).astype(BF16)


def _dot(a, b):
    return jnp.dot(a, b, preferred_element_type=F32)


def _dot_nt(a, b):
    return lax.dot_general(a, b, (((1,), (1,)), ((), ())), preferred_element_type=F32)


def _const_spec(shape):
    return pl.BlockSpec(shape, lambda *_: (0,) * len(shape))


def _inproj_kernel(x_ref, g_ref, w_ref, q_ref, k_ref, v_ref, u_ref):
    xn = _rms(x_ref[...], g_ref[...]).astype(BF16)
    aw = q_ref.shape[-1]
    q_ref[...] = _dot(xn, w_ref[:, 0:aw])
    k_ref[...] = _dot(xn, w_ref[:, aw:2 * aw])
    v_ref[...] = _dot(xn, w_ref[:, 2 * aw:3 * aw])
    u_ref[...] = _dot(xn, w_ref[:, 3 * aw:])


def _inproj(x, g, w_bf, att_w, bm):
    m, d = x.shape
    n = w_bf.shape[1]
    sw = n - 3 * att_w
    row = lambda i: (i, 0)
    return pl.pallas_call(
        _inproj_kernel,
        grid=(m // bm,),
        in_specs=[pl.BlockSpec((bm, d), row), _const_spec((1, d)), _const_spec((d, n))],
        out_specs=[pl.BlockSpec((bm, att_w), row)] * 3 + [pl.BlockSpec((bm, sw), row)],
        out_shape=[jax.ShapeDtypeStruct((m, att_w), F32)] * 3 + [jax.ShapeDtypeStruct((m, sw), F32)],
        compiler_params=_cparams(1),
        name="inproj",
    )(x, g, w_bf)


def _inproj_seq_kernel(x_ref, g_ref, wq_ref, wkt_ref, wvt_ref, wu_ref, q_ref, kt_ref, vt_ref, u_ref):
    xn = _rms(x_ref[...], g_ref[...]).astype(BF16)
    q_ref[...] = _dot(xn, wq_ref[...])
    kt_ref[...] = _dot_nt(wkt_ref[...], xn)
    vt_ref[...] = _dot_nt(wvt_ref[...], xn)
    u_ref[...] = _dot(xn, wu_ref[...])


def _inproj_seq(x, g, w_bf, att_w, b, t, bm):
    m, d = x.shape
    n = w_bf.shape[1]
    sw = n - 3 * att_w
    nb = t // bm
    row = lambda i: (i, 0)
    tspec = pl.BlockSpec((att_w, bm), lambda i: (i // nb, i % nb))
    wq, wk, wv, wu = (w_bf[:, 0:att_w], w_bf[:, att_w:2 * att_w], w_bf[:, 2 * att_w:3 * att_w],
                      w_bf[:, 3 * att_w:])
    return pl.pallas_call(
        _inproj_seq_kernel,
        grid=(m // bm,),
        in_specs=[pl.BlockSpec((bm, d), row), _const_spec((1, d)), _const_spec((d, att_w)),
                  _const_spec((att_w, d)), _const_spec((att_w, d)), _const_spec((d, sw))],
        out_specs=[pl.BlockSpec((bm, att_w), row), tspec, tspec, pl.BlockSpec((bm, sw), row)],
        out_shape=[jax.ShapeDtypeStruct((m, att_w), F32), jax.ShapeDtypeStruct((b * att_w, t), F32),
                   jax.ShapeDtypeStruct((b * att_w, t), F32), jax.ShapeDtypeStruct((m, sw), F32)],
        compiler_params=_cparams(1),
        name="inproj_seq",
    )(x, g, wq, wk.T, wv.T, wu)


def _attn_kernel(bias_ref, q_ref, ktin_ref, vtin_ref, uu_ref, o_ref, kb_ref, vt_ref, acc_ref, *, bq, bk, qw, t):
    hp = pl.program_id(1)
    qi = pl.program_id(2)
    nk = t // bk
    r = bq // bk
    bias_lane = [(1 - hh) * HEAD_DIM for hh in range(HEADS_PER_BLOCK)]

    def own_lanes(lane, hh):
        return (lane >= hh * HEAD_DIM) & (lane < (hh + 1) * HEAD_DIM)

    @pl.when(qi == 0)
    def _prep():
        lane = lax.broadcasted_iota(jnp.int32, (bk, LANES), 1)
        for j in range(nk):
            cols = slice(j * bk, (j + 1) * bk)
            kblk = ktin_ref[:, cols].T
            for hh in range(HEADS_PER_BLOCK):
                ones = (lane == bias_lane[hh]) | (lane == bias_lane[hh] + 1)
                kb_ref[hh, j] = jnp.where(own_lanes(lane, hh), kblk,
                                          jnp.where(ones, 1.0, 0.0)).astype(BF16)
            vt_ref[j] = vtin_ref[:, cols].astype(BF16)

    lane_q = lax.broadcasted_iota(jnp.int32, (bq, LANES), 1)
    q = q_ref[...] * (HEAD_DIM ** -0.5 * LOG2E)
    qa = []
    for hh in range(HEADS_PER_BLOCK):
        b2 = jnp.full((bq, LANES), bias_ref[HEADS_PER_BLOCK * hp + hh] * LOG2E, F32)
        b_hi = b2.astype(BF16).astype(F32)
        aug = jnp.where(lane_q == bias_lane[hh], b_hi,
                        jnp.where(lane_q == bias_lane[hh] + 1, b2 - b_hi, 0.0))
        qa.append(jnp.where(own_lanes(lane_q, hh), q, aug).astype(BF16))
    uu = uu_ref[...]

    nqs = bq // qw

    def tiles(first_block, carries, diagonal):
        order = []
        for d in reversed(range(r)):
            for hh in range(HEADS_PER_BLOCK):
                for qs in range(nqs):
                    if not diagonal or (qs + 1) * qw - 1 > d * bk:
                        order.append((hh, d, qs))
        n = len(order)
        carries = list(carries)
        zz = [None] * n
        incl = [None] * n

        def visible(d, qs):
            if not diagonal or qs * qw >= (d + 1) * bk:
                return None
            return (lax.broadcasted_iota(jnp.int32, (bk, qw), 0) + d * bk
                    < lax.broadcasted_iota(jnp.int32, (bk, qw), 1) + qs * qw)

        def logits(i):
            hh, d, qs = order[i]
            zz[i] = _dot_nt(kb_ref[hh, first_block + d], qa[hh][qs * qw:(qs + 1) * qw, :])

        def suffix_sums(i):
            hh, d, qs = order[i]
            p = _softplus2(zz[i])
            vis = visible(d, qs)
            if vis is not None:
                p = jnp.where(vis, p, 0.0)
            incl[i] = _dot(uu, p.astype(BF16))

        def weigh(i):
            hh, d, qs = order[i]
            c = hh * nqs + qs
            w = jnp.exp2(zz[i] - incl[i] - carries[c])
            vis = visible(d, qs)
            if vis is not None:
                w = jnp.where(vis, w, 0.0)
            carries[c] = carries[c] + incl[i][0:1, :]
            acc_ref[hh, :, qs * qw:(qs + 1) * qw] += _dot(
                vt_ref[first_block + d, hh * HEAD_DIM:(hh + 1) * HEAD_DIM, :], w.astype(BF16))
            zz[i] = incl[i] = None

        for s in range(n + WEIGH_LAG):
            if s < n:
                logits(s)
            if 0 <= s - SUFFIX_LAG < n:
                suffix_sums(s - SUFFIX_LAG)
            if 0 <= s - WEIGH_LAG < n:
                weigh(s - WEIGH_LAG)
        return tuple(carries)

    acc_ref[...] = jnp.zeros_like(acc_ref)
    zero = jnp.zeros((1, qw), F32)
    carries = tiles(qi * r, (zero,) * (HEADS_PER_BLOCK * nqs), True)
    lax.fori_loop(0, qi, lambda it, c: tiles((qi - 1 - it) * r, c, False), carries)
    o_ref[...] = acc_ref[...].reshape(HEADS_PER_BLOCK * HEAD_DIM, bq).T


def _suffix_ones(bk):
    return (jnp.arange(bk)[None, :] >= jnp.arange(bk)[:, None]).astype(BF16)


def _prompt_attention(q, kt, vt, bias, b, t, bq=1024, bk=128, qw=256):
    m, aw = q.shape
    nq = t // bq
    nk = t // bk
    hp = aw // LANES
    qspec = pl.BlockSpec((bq, LANES), lambda bi, h, i: (bi * nq + i, h))
    kvspec = pl.BlockSpec((LANES, t), lambda bi, h, i: (bi * hp + h, 0))
    return pl.pallas_call(
        functools.partial(_attn_kernel, bq=bq, bk=bk, qw=qw, t=t),
        grid=(b, hp, nq),
        in_specs=[pl.BlockSpec(memory_space=pltpu.SMEM), qspec, kvspec, kvspec,
                  _const_spec((bk, bk))],
        out_specs=qspec,
        out_shape=jax.ShapeDtypeStruct((m, aw), F32),
        scratch_shapes=[pltpu.VMEM((HEADS_PER_BLOCK, nk, bk, LANES), BF16),
                        pltpu.VMEM((nk, LANES, bk), BF16),
                        pltpu.VMEM((HEADS_PER_BLOCK, HEAD_DIM, bq), F32)],
        compiler_params=_cparams(3),
        name="prompt_attention",
    )(bias, q, kt, vt, _suffix_ones(bk))


def _decode_attn_kernel(pt_ref, q_ref, bias_ref, ll_ref, *refs, n_pages_step, n_heads):
    del pt_ref
    g = n_pages_step
    k_refs = refs[:g]
    v_refs = refs[g:2 * g]
    o_ref = refs[2 * g]
    acc_ref, carry_ref = refs[2 * g + 1:]
    s = pl.program_id(1)
    aw = q_ref.shape[-1]

    @pl.when(s == 0)
    def _init():
        acc_ref[...] = jnp.zeros_like(acc_ref)
        carry_ref[...] = jnp.zeros_like(carry_ref)

    head_of_lane = lax.shift_right_logical(lax.broadcasted_iota(jnp.int32, (n_heads, aw), 1), 6)
    own = head_of_lane == lax.broadcasted_iota(jnp.int32, (n_heads, aw), 0)
    qrows = jnp.where(own, q_ref[0] * (HEAD_DIM ** -0.5 * LOG2E), 0.0).astype(BF16)
    bias2 = bias_ref[...] * LOG2E
    page = k_refs[0].shape[-1]
    paged = lambda ref: ref[0].reshape(aw, page).astype(BF16)
    rows = lambda a, i: a[i * n_heads:(i + 1) * n_heads]

    zz = jnp.concatenate([_dot(qrows, paged(k_refs[i])) + bias2 for i in range(g)], axis=0)
    hi, lo = _split_bf16(_softplus2(zz))
    cs = _dot(jnp.concatenate([hi, lo], axis=1), ll_ref[...])
    incl, tot = cs[:, :page], cs[:, page:]
    carry = carry_ref[...]
    carries = []
    for i in range(g):
        carries.append(carry)
        carry = carry + rows(tot, i)
    carry_ref[...] = carry
    w = jnp.exp2(zz - incl - jnp.concatenate(carries, axis=0))
    acc = acc_ref[...]
    for i in range(g):
        acc = acc + _dot_nt(rows(w, i).astype(BF16), paged(v_refs[i]))
    acc_ref[...] = acc

    @pl.when(s == pl.num_programs(1) - 1)
    def _fin():
        o_ref[0] = jnp.sum(jnp.where(own, acc_ref[...], 0.0), axis=0, keepdims=True)


def _decode_ones(page):
    sp = jnp.arange(2 * page)[:, None] % page
    s = jnp.arange(page)[None, :]
    low = (sp >= s).astype(BF16)
    return jnp.concatenate([low, jnp.ones((2 * page, page), BF16)], axis=1)


def _decode_attention(q, cache_kt, cache_vt, page_table, bias, n_pages_step=16):
    bs, aw = q.shape
    n_phys, n_heads, hd, page = cache_kt.shape
    npg = page_table.shape[1]
    g = n_pages_step
    steps = npg // g

    def page_spec(i):
        return pl.BlockSpec((1, n_heads, hd, page),
                            lambda b, s, pt: (pt[b, npg - 1 - (s * g + i)], 0, 0, 0))

    qspec = pl.BlockSpec((1, 1, aw), lambda b, s, pt: (b, 0, 0))
    grid_spec = pltpu.PrefetchScalarGridSpec(
        num_scalar_prefetch=1,
        grid=(bs, steps),
        in_specs=[qspec,
                  pl.BlockSpec((n_heads, LANES), lambda b, s, pt: (0, 0)),
                  pl.BlockSpec((2 * page, 2 * page), lambda b, s, pt: (0, 0))]
                 + [page_spec(i) for i in range(g)] * 2,
        out_specs=qspec,
        scratch_shapes=[pltpu.VMEM((n_heads, aw), F32), pltpu.VMEM((n_heads, page), F32)],
    )
    bias_b = jnp.broadcast_to(bias[:, None], (n_heads, LANES))
    out = pl.pallas_call(
        functools.partial(_decode_attn_kernel, n_pages_step=g, n_heads=n_heads),
        grid_spec=grid_spec,
        out_shape=jax.ShapeDtypeStruct((bs, 1, aw), F32),
        compiler_params=_cparams(2),
        name="decode_attention",
    )(page_table, q.reshape(bs, 1, aw), bias_b, _decode_ones(page),
      *([cache_kt] * g), *([cache_vt] * g))
    return out.reshape(bs, aw)


def _ssm_prep_kernel(lr_ref, li_ref, ldt_ref, btr_ref, bti_ref, ctr_ref, cti_ref,
                     wbr_ref, wbi_ref, wcr_ref, wci_ref, lpr_ref, lpi_ref, amr_ref, ami_ref,
                     *, log2_cg, log2_p):
    i = pl.program_id(0)
    blk = lr_ref.shape[-1]
    lam_r = lr_ref[...]
    lam_i = li_ref[...]
    dt = jnp.exp(ldt_ref[...])
    mag = jnp.exp(lam_r * dt)
    ang = lam_i * dt
    lb_r = mag * jnp.cos(ang)
    lb_i = mag * jnp.sin(ang)
    nr = lb_r - 1.0
    den = lam_r * lam_r + lam_i * lam_i
    cr = (nr * lam_r + lb_i * lam_i) / den
    ci = (lb_i * lam_r - nr * lam_i) / den

    shp = btr_ref.shape
    rg = lax.shift_right_logical(lax.broadcasted_iota(jnp.int32, shp, 0), log2_cg)
    cg = lax.shift_right_logical(lax.broadcasted_iota(jnp.int32, shp, 1) + i * blk, log2_p)
    same = rg == cg
    btr = btr_ref[...]
    bti = bti_ref[...]
    wbr_ref[...] = jnp.where(same, cr * btr - ci * bti, 0.0).astype(BF16)
    wbi_ref[...] = jnp.where(same, cr * bti + ci * btr, 0.0).astype(BF16)

    shp = ctr_ref.shape
    rg = lax.shift_right_logical(lax.broadcasted_iota(jnp.int32, shp, 0) + i * blk, log2_p)
    cg = lax.shift_right_logical(lax.broadcasted_iota(jnp.int32, shp, 1), log2_cg)
    same = rg == cg
    wcr_ref[...] = jnp.where(same, ctr_ref[...], 0.0).astype(BF16)
    wci_ref[...] = jnp.where(same, -cti_ref[...], 0.0).astype(BF16)

    rows = lax.broadcasted_iota(jnp.int32, (SUBLANES, blk), 0)
    pr, pi = lb_r, lb_i
    lpr = jnp.zeros((SUBLANES, blk), F32)
    lpi = jnp.zeros((SUBLANES, blk), F32)
    slot = 0
    for n in range(1, SUBLANES + 1):
        lpr = jnp.where(rows == n - 1, pr, lpr)
        lpi = jnp.where(rows == n - 1, pi, lpi)
        if n in (1, 2, 4):
            amr_ref[slot] = jnp.where(rows >= n, pr, 0.0)
            ami_ref[slot] = jnp.where(rows >= n, pi, 0.0)
            slot += 1
        pr, pi = pr * lb_r - pi * lb_i, pr * lb_i + pi * lb_r
    lpr_ref[...] = lpr
    lpi_ref[...] = lpi


def _ssm_prep(lam_re, lam_im, log_dt, b_re, b_im, c_re, c_im, nblk=4):
    g, p = lam_re.shape
    cgs = b_re.shape[-1]
    gp, gc = g * p, g * cgs
    blk = gp // nblk
    flat = lambda a: a.reshape(1, gp)
    ldt = flat(jnp.broadcast_to(log_dt[:, None], (g, p)))
    bt = lambda a: jnp.tile(a.transpose(0, 2, 1).reshape(gc, p), (1, g))
    ct = lambda a: jnp.tile(a.transpose(0, 2, 1).reshape(gp, cgs), (1, g))
    lane_blk = lambda shape: pl.BlockSpec(shape, lambda i: (0,) * (len(shape) - 1) + (i,))
    return pl.pallas_call(
        functools.partial(_ssm_prep_kernel, log2_cg=int(math.log2(cgs)), log2_p=int(math.log2(p))),
        grid=(nblk,),
        in_specs=[lane_blk((1, blk))] * 3 + [lane_blk((gc, blk))] * 2
                 + [pl.BlockSpec((blk, gc), lambda i: (i, 0))] * 2,
        out_specs=[lane_blk((gc, blk))] * 2 + [pl.BlockSpec((blk, gc), lambda i: (i, 0))] * 2
                  + [lane_blk((SUBLANES, blk))] * 2 + [lane_blk((3, SUBLANES, blk))] * 2,
        out_shape=[jax.ShapeDtypeStruct((gc, gp), BF16)] * 2 + [jax.ShapeDtypeStruct((gp, gc), BF16)] * 2
                  + [jax.ShapeDtypeStruct((SUBLANES, gp), F32)] * 2
                  + [jax.ShapeDtypeStruct((3, SUBLANES, gp), F32)] * 2,
        compiler_params=_cparams(1),
        name="ssm_prep",
    )(flat(lam_re), flat(lam_im), ldt, bt(b_re), bt(b_im), ct(c_re), ct(c_im))


def _ssm_tail(hr, hi, u, wcr_ref, wci_ref, dsk_ref, wglu_ref):
    y = _dot(hr.astype(BF16), wcr_ref[...]) + _dot(hi.astype(BF16), wci_ref[...]) + dsk_ref[...] * u
    zs = _gelu(y)
    return zs * _sigmoid(_dot(zs.astype(BF16), wglu_ref[...]))


def _ssm_scan_kernel(u_ref, wbr_ref, wbi_ref, wcr_ref, wci_ref, lpr_ref, lpi_ref, amr_ref, ami_ref,
                     dsk_ref, wglu_ref, o_ref, hr_ref, hi_ref, xr_ref, xi_ref, cr_ref, ci_ref,
                     *, lane_blk):
    tc = pl.program_id(1)
    lt = u_ref.shape[0]
    gp = wbr_ref.shape[1]
    ng = lt // SUBLANES

    @pl.when(tc == 0)
    def _init():
        cr_ref[...] = jnp.zeros_like(cr_ref)
        ci_ref[...] = jnp.zeros_like(ci_ref)

    u = u_ref[...]
    ub = u.astype(BF16)
    xr = _dot(ub, wbr_ref[...]).reshape(ng, SUBLANES, gp)
    xi = _dot(ub, wbi_ref[...]).reshape(ng, SUBLANES, gp)
    for idx, d in enumerate((1, 2, 4)):
        ar = amr_ref[idx]
        ai = ami_ref[idx]
        sr = pltpu.roll(xr, d, axis=1)
        si = pltpu.roll(xi, d, axis=1)
        xr, xi = xr + ar * sr - ai * si, xi + ar * si + ai * sr
    xr_ref[...] = xr
    xi_ref[...] = xi

    for lb in range(gp // lane_blk):
        ls = slice(lb * lane_blk, (lb + 1) * lane_blk)
        lpr = lpr_ref[:, ls]
        lpi = lpi_ref[:, ls]

        def body(i, c, ls=ls, lpr=lpr, lpi=lpi):
            hr, hi = c
            r = xr_ref[i, :, ls] + lpr * hr - lpi * hi
            im = xi_ref[i, :, ls] + lpr * hi + lpi * hr
            xr_ref[i, :, ls] = r
            xi_ref[i, :, ls] = im
            last = SUBLANES - 1
            return (jnp.broadcast_to(r[last:, :], r.shape), jnp.broadcast_to(im[last:, :], im.shape))

        hr, hi = lax.fori_loop(0, ng, body, (cr_ref[:, ls], ci_ref[:, ls]))
        cr_ref[:, ls] = hr
        ci_ref[:, ls] = hi

    hr_ref[0] = cr_ref[0:1, :]
    hi_ref[0] = ci_ref[0:1, :]
    o_ref[...] = _ssm_tail(xr_ref[...].reshape(lt, gp), xi_ref[...].reshape(lt, gp), u,
                           wcr_ref, wci_ref, dsk_ref, wglu_ref)


def _ssm_prompt(u, prep, dsk, wglu_bf, b, t, lt=256, lane_blk=1024):
    wbr, wbi, wcr, wci, lpr, lpi, amr, ami = prep
    m, sw = u.shape
    gp = wbr.shape[1]
    nt = t // lt
    row = pl.BlockSpec((lt, sw), lambda bi, i: (bi * nt + i, 0))
    hspec = pl.BlockSpec((1, 1, gp), lambda bi, i: (bi, 0, 0))
    return pl.pallas_call(
        functools.partial(_ssm_scan_kernel, lane_blk=lane_blk),
        grid=(b, nt),
        in_specs=[row, _const_spec((sw, gp)), _const_spec((sw, gp)), _const_spec((gp, sw)),
                  _const_spec((gp, sw)), _const_spec((SUBLANES, gp)), _const_spec((SUBLANES, gp)),
                  _const_spec((3, SUBLANES, gp)), _const_spec((3, SUBLANES, gp)),
                  _const_spec((1, sw)), _const_spec((sw, sw))],
        out_specs=[row, hspec, hspec],
        out_shape=[jax.ShapeDtypeStruct((m, sw), F32), jax.ShapeDtypeStruct((b, 1, gp), F32),
                   jax.ShapeDtypeStruct((b, 1, gp), F32)],
        scratch_shapes=[pltpu.VMEM((lt // SUBLANES, SUBLANES, gp), F32)] * 2
                       + [pltpu.VMEM((SUBLANES, gp), F32)] * 2,
        compiler_params=_cparams(2),
        name="ssm_scan",
    )(u, wbr, wbi, wcr, wci, lpr, lpi, amr, ami, dsk, wglu_bf)


def _ssm_step_kernel(u_ref, h0r_ref, h0i_ref, wbr_ref, wbi_ref, wcr_ref, wci_ref, lpr_ref, lpi_ref,
                     dsk_ref, wglu_ref, o_ref, hr_ref, hi_ref):
    u = u_ref[...]
    ub = u.astype(BF16)
    lr = lpr_ref[0:1, :]
    li = lpi_ref[0:1, :]
    h0r = h0r_ref[...]
    h0i = h0i_ref[...]
    hr = _dot(ub, wbr_ref[...]) + (lr * h0r - li * h0i)
    hi = _dot(ub, wbi_ref[...]) + (lr * h0i + li * h0r)
    hr_ref[...] = hr
    hi_ref[...] = hi
    o_ref[...] = _ssm_tail(hr, hi, u, wcr_ref, wci_ref, dsk_ref, wglu_ref)


def _ssm_step(u, h0r, h0i, prep, dsk, wglu_bf):
    wbr, wbi, wcr, wci, lpr, lpi, _, _ = prep
    m, sw = u.shape
    gp = wbr.shape[1]
    return pl.pallas_call(
        _ssm_step_kernel,
        out_shape=[jax.ShapeDtypeStruct((m, sw), F32), jax.ShapeDtypeStruct((m, gp), F32),
                   jax.ShapeDtypeStruct((m, gp), F32)],
        compiler_params=pltpu.CompilerParams(vmem_limit_bytes=VMEM_LIMIT),
        name="ssm_step",
    )(u, h0r, h0i, wbr, wbi, wcr, wci, lpr, lpi, dsk, wglu_bf)


def _merge_kernel(x_ref, att_ref, ssm_ref, ga_ref, gs_ref, wo_ref, gp_ref, o_ref):
    an = _rms(att_ref[...], ga_ref[...]).astype(BF16)
    sn = _rms(ssm_ref[...], gs_ref[...]).astype(BF16)
    aw = an.shape[-1]
    y = _dot(an, wo_ref[0:aw, :]) + _dot(sn, wo_ref[aw:, :])
    o_ref[...] = x_ref[...] + _rms(y, gp_ref[...])


def _merge(x, att, ssm, ga, gs, wo_bf, gp, bm):
    m, d = x.shape
    aw, sw = att.shape[1], ssm.shape[1]
    row = lambda w: pl.BlockSpec((bm, w), lambda i: (i, 0))
    return pl.pallas_call(
        _merge_kernel,
        grid=(m // bm,),
        in_specs=[row(d), row(aw), row(sw), _const_spec((1, aw)), _const_spec((1, sw)),
                  _const_spec((aw + sw, d)), _const_spec((1, d))],
        out_specs=row(d),
        out_shape=jax.ShapeDtypeStruct((m, d), F32),
        compiler_params=_cparams(1),
        name="merge_outproj",
    )(x, att, ssm, ga, gs, wo_bf, gp)


def _ffn_chunks(f, fc):
    return [slice(c * fc, (c + 1) * fc) for c in range(f // fc)]


def _ffn_seq_kernel(x_ref, g_ref, wg_ref, wu_ref, cw_ref, cb_ref, wd_ref, gpost_ref,
                    o_ref, cn_ref, gs_ref, *, fc, blocks_per_seq):
    i = pl.program_id(0)
    bm = x_ref.shape[0]
    hist = SUBLANES
    first = (i % blocks_per_seq) == 0

    @pl.when(first)
    def _zero_hist():
        gs_ref[0:hist, :] = jnp.zeros((hist, gs_ref.shape[1]), F32)

    @pl.when(jnp.logical_not(first))
    def _keep_hist():
        gs_ref[0:hist, :] = gs_ref[bm:bm + hist, :]

    x = x_ref[...]
    xn = _rms(x, g_ref[...]).astype(BF16)
    acc = jnp.zeros(x.shape, F32)
    for cs in _ffn_chunks(wg_ref.shape[1], fc):
        gate = _dot(xn, wg_ref[:, cs])
        up = _dot(xn, wu_ref[:, cs])
        gs_ref[hist:hist + bm, cs] = gate
        g1 = gs_ref[hist - 1:hist - 1 + bm, cs]
        g2 = gs_ref[hist - 2:hist - 2 + bm, cs]
        conv = cb_ref[:, cs] + g2 * cw_ref[0:1, cs] + g1 * cw_ref[1:2, cs] + gate * cw_ref[2:3, cs]
        acc = acc + _dot((_gelu(conv) * up).astype(BF16), wd_ref[cs, :])
    cn_ref[0] = gs_ref[hist + bm - 2:hist + bm, :]
    o_ref[...] = x + _rms(acc, gpost_ref[...])


def _ffn_weight_specs(d, f):
    return [_const_spec((1, d)), _const_spec((d, f)), _const_spec((d, f)), _const_spec((3, f)),
            _const_spec((1, f)), _const_spec((f, d)), _const_spec((1, d))]


def _ffn_seq(x, g, wg_bf, wu_bf, cw, cb, wd_bf, gpost, b, t, bm=256, fc=1408):
    m, d = x.shape
    f = wg_bf.shape[1]
    bps = t // bm
    row = pl.BlockSpec((bm, d), lambda i: (i, 0))
    return pl.pallas_call(
        functools.partial(_ffn_seq_kernel, fc=fc, blocks_per_seq=bps),
        grid=(m // bm,),
        in_specs=[row] + _ffn_weight_specs(d, f),
        out_specs=[row, pl.BlockSpec((1, 2, f), lambda i: (i // bps, 0, 0))],
        out_shape=[jax.ShapeDtypeStruct((m, d), F32), jax.ShapeDtypeStruct((b, 2, f), F32)],
        scratch_shapes=[pltpu.VMEM((SUBLANES + bm, f), F32)],
        compiler_params=_cparams(1),
        name="convffn_seq",
    )(x, g, wg_bf, wu_bf, cw, cb, wd_bf, gpost)


def _ffn_step_kernel(x_ref, g_ref, wg_ref, wu_ref, cw_ref, cb_ref, wd_ref, gpost_ref, p0_ref, p1_ref,
                     o_ref, gate_ref, *, fc):
    x = x_ref[...]
    xn = _rms(x, g_ref[...]).astype(BF16)
    acc = jnp.zeros(x.shape, F32)
    for cs in _ffn_chunks(wg_ref.shape[1], fc):
        gate = _dot(xn, wg_ref[:, cs])
        up = _dot(xn, wu_ref[:, cs])
        gate_ref[:, cs] = gate
        conv = (cb_ref[:, cs] + p0_ref[:, cs] * cw_ref[0:1, cs] + p1_ref[:, cs] * cw_ref[1:2, cs]
                + gate * cw_ref[2:3, cs])
        acc = acc + _dot((_gelu(conv) * up).astype(BF16), wd_ref[cs, :])
    o_ref[...] = x + _rms(acc, gpost_ref[...])


def _ffn_step(x, g, wg_bf, wu_bf, cw, cb, wd_bf, gpost, p0, p1, fc=1408):
    m, d = x.shape
    f = wg_bf.shape[1]
    return pl.pallas_call(
        functools.partial(_ffn_step_kernel, fc=fc),
        out_shape=[jax.ShapeDtypeStruct((m, d), F32), jax.ShapeDtypeStruct((m, f), F32)],
        compiler_params=pltpu.CompilerParams(vmem_limit_bytes=VMEM_LIMIT),
        name="convffn_step",
    )(x, g, wg_bf, wu_bf, cw, cb, wd_bf, gpost, p0, p1)


def kernel(x_prompt, x_sample, cache_k, cache_v, state_ssm_re, state_ssm_im, state_ffn_conv, page_table,
           g_pre_mix, w_in, sb_bias, g_att_out, lam_re, lam_im, log_dt, b_re, b_im, c_re, c_im, d_skip,
           w_glu, g_ssm_out, w_out, g_post_mix, g_pre_ffn, w_gate, w_up, conv_w, conv_b, w_down,
           g_post_ffn):
    depth = w_in.shape[0]
    assert depth == 1, "single-layer step"
    b, t, d = x_prompt.shape
    bs, ts, _ = x_sample.shape
    assert ts == 1, "sample group advances one token per sequence"
    n_heads, hd = cache_k.shape[-2:]
    assert hd == HEAD_DIM
    aw = n_heads * hd
    g, p = lam_re.shape[1:]
    l = 0

    row = lambda a: a[l].reshape(1, -1)
    w_in_bf = w_in[l].astype(BF16)
    wglu_bf = w_glu[l].astype(BF16)
    wo_bf = w_out[l].astype(BF16)
    wg_bf = w_gate[l].astype(BF16)
    wu_bf = w_up[l].astype(BF16)
    wd_bf = w_down[l].astype(BF16)
    bias = sb_bias[l]
    dsk = row(d_skip)
    prep = _ssm_prep(lam_re[l], lam_im[l], log_dt[l], b_re[l], b_im[l], c_re[l], c_im[l])

    xp = x_prompt.reshape(b * t, d)
    qp, ktp, vtp, up = _inproj_seq(xp, row(g_pre_mix), w_in_bf, aw, b, t, bm=512)
    att_p = _prompt_attention(qp, ktp, vtp, bias, b, t)
    ssm_p, hrp, hip = _ssm_prompt(up, prep, dsk, wglu_bf, b, t)
    xp1 = _merge(xp, att_p, ssm_p, row(g_att_out), row(g_ssm_out), wo_bf, row(g_post_mix), bm=512)
    yp, conv_p = _ffn_seq(xp1, row(g_pre_ffn), wg_bf, wu_bf, conv_w[l], row(conv_b), wd_bf,
                          row(g_post_ffn), b, t)

    xs = x_sample.reshape(bs, d)
    qs, ks, vs, us = _inproj(xs, row(g_pre_mix), w_in_bf, aw, bm=bs)
    att_s = _decode_attention(qs, cache_k[l].transpose(0, 2, 3, 1), cache_v[l].transpose(0, 2, 3, 1),
                              page_table, bias)
    ssm_s, hrs, his = _ssm_step(us, state_ssm_re[l].reshape(bs, g * p), state_ssm_im[l].reshape(bs, g * p),
                                prep, dsk, wglu_bf)
    xs1 = _merge(xs, att_s, ssm_s, row(g_att_out), row(g_ssm_out), wo_bf, row(g_post_mix), bm=bs)
    prev = state_ffn_conv[l]
    ys, gate_s = _ffn_step(xs1, row(g_pre_ffn), wg_bf, wu_bf, conv_w[l], row(conv_b), wd_bf,
                           row(g_post_ffn), prev[:, 0], prev[:, 1])
    conv_s = jnp.stack([prev[:, 1], gate_s], axis=1)

    heads = lambda a, n, s: a.reshape(1, n, s, n_heads, hd)
    heads_t = lambda a: a.reshape(1, b, n_heads, hd, t).transpose(0, 1, 4, 2, 3)
    state = lambda a, n: a.reshape(1, n, g, p)
    return (yp.reshape(b, t, d), ys.reshape(bs, 1, d),
            heads_t(ktp), heads_t(vtp), heads(ks, bs, 1), heads(vs, bs, 1),
            state(hrp, b), state(hip, b), state(hrs, bs), state(his, bs),
            conv_p[None], conv_s[None])
```

```python
import functools
import math

import jax
import jax.numpy as jnp
from jax import lax
from jax.experimental import pallas as pl
from jax.experimental.pallas import tpu as pltpu

F32 = jnp.float32
BF16 = jnp.bfloat16

RMS_EPS = 1e-6
HEAD_DIM = 64
LANES = 128
SUBLANES = 8
HEADS_PER_BLOCK = LANES // HEAD_DIM
SQRT_2_OVER_PI = math.sqrt(2.0 / math.pi)
LOG2E = math.log2(math.e)
SUFFIX_LAG = 3
WEIGH_LAG = 7
MXU_SCAN = 4
SCAN_SHIFTS = (1, 2, 4)
SCAN_LEVELS = 2
V7X_VMEM_BYTES = 64 * 1024 * 1024
VMEM_LIMIT = V7X_VMEM_BYTES - 8 * 1024 * 1024


def _cparams(n_axes):
    return pltpu.CompilerParams(dimension_semantics=("arbitrary",) * n_axes,
                                vmem_limit_bytes=VMEM_LIMIT)


def _rms(x, g):
    return x * lax.rsqrt(jnp.mean(x * x, axis=-1, keepdims=True) + RMS_EPS) * g


def _gelu(x):
    return 0.5 * x * (1.0 + jnp.tanh(SQRT_2_OVER_PI * (x + 0.044715 * (x * x * x))))


def _sigmoid(x):
    return 1.0 / (1.0 + jnp.exp(-x))


def _softplus2(zz):
    return jnp.maximum(zz, 0.0) + jnp.log(1.0 + jnp.exp2(-jnp.abs(zz))) * LOG2E


def _split_bf16(x):
    hi = x.astype(BF16)
    return hi, (x - hi.astype(F32)).astype(BF16)


def _dot(a, b):
    return jnp.dot(a, b, preferred_element_type=F32)


def _dot_nt(a, b):
    return lax.dot_general(a, b, (((1,), (1,)), ((), ())), preferred_element_type=F32)


def _const_spec(shape):
    return pl.BlockSpec(shape, lambda *_: (0,) * len(shape))


def _inproj_kernel(x_ref, g_ref, w_ref, q_ref, k_ref, v_ref, u_ref):
    xn = _rms(x_ref[...], g_ref[...]).astype(BF16)
    aw = q_ref.shape[-1]
    q_ref[...] = _dot(xn, w_ref[:, 0:aw])
    k_ref[...] = _dot(xn, w_ref[:, aw:2 * aw])
    v_ref[...] = _dot(xn, w_ref[:, 2 * aw:3 * aw])
    u_ref[...] = _dot(xn, w_ref[:, 3 * aw:])


def _inproj(x, g, w_bf, att_w, bm):
    m, d = x.shape
    n = w_bf.shape[1]
    sw = n - 3 * att_w
    row = lambda i: (i, 0)
    return pl.pallas_call(
        _inproj_kernel,
        grid=(m // bm,),
        in_specs=[pl.BlockSpec((bm, d), row), _const_spec((1, d)), _const_spec((d, n))],
        out_specs=[pl.BlockSpec((bm, att_w), row)] * 3 + [pl.BlockSpec((bm, sw), row)],
        out_shape=[jax.ShapeDtypeStruct((m, att_w), F32)] * 3 + [jax.ShapeDtypeStruct((m, sw), F32)],
        compiler_params=_cparams(1),
        name="inproj",
    )(x, g, w_bf)


def _inproj_seq_kernel(x_ref, g_ref, wq_ref, wkt_ref, wvt_ref, wu_ref, q_ref, kt_ref, vt_ref, u_ref):
    xn = _rms(x_ref[...], g_ref[...]).astype(BF16)
    q_ref[...] = _dot(xn, wq_ref[...])
    kt_ref[...] = _dot_nt(wkt_ref[...], xn)
    vt_ref[...] = _dot_nt(wvt_ref[...], xn)
    u_ref[...] = _dot(xn, wu_ref[...])


def _inproj_seq(x, g, w_bf, att_w, b, t, bm):
    m, d = x.shape
    n = w_bf.shape[1]
    sw = n - 3 * att_w
    nb = t // bm
    row = lambda i: (i, 0)
    tspec = pl.BlockSpec((att_w, bm), lambda i: (i // nb, i % nb))
    wq, wk, wv, wu = (w_bf[:, 0:att_w], w_bf[:, att_w:2 * att_w], w_bf[:, 2 * att_w:3 * att_w],
                      w_bf[:, 3 * att_w:])
    return pl.pallas_call(
        _inproj_seq_kernel,
        grid=(m // bm,),
        in_specs=[pl.BlockSpec((bm, d), row), _const_spec((1, d)), _const_spec((d, att_w)),
                  _const_spec((att_w, d)), _const_spec((att_w, d)), _const_spec((d, sw))],
        out_specs=[pl.BlockSpec((bm, att_w), row), tspec, tspec, pl.BlockSpec((bm, sw), row)],
        out_shape=[jax.ShapeDtypeStruct((m, att_w), F32), jax.ShapeDtypeStruct((b * att_w, t), F32),
                   jax.ShapeDtypeStruct((b * att_w, t), F32), jax.ShapeDtypeStruct((m, sw), F32)],
        compiler_params=_cparams(1),
        name="inproj_seq",
    )(x, g, wq, wk.T, wv.T, wu)


def _attn_kernel(bias_ref, q_ref, ktin_ref, vtin_ref, uu_ref, o_ref, kb_ref, vt_ref, acc_ref, *, bq, bk, qw, t):
    hp = pl.program_id(1)
    qi = pl.program_id(2)
    nk = t // bk
    r = bq // bk
    bias_lane = [(1 - hh) * HEAD_DIM for hh in range(HEADS_PER_BLOCK)]

    def own_lanes(lane, hh):
        return (lane >= hh * HEAD_DIM) & (lane < (hh + 1) * HEAD_DIM)

    @pl.when(qi == 0)
    def _prep():
        lane = lax.broadcasted_iota(jnp.int32, (bk, LANES), 1)
        for j in range(nk):
            cols = slice(j * bk, (j + 1) * bk)
            kblk = ktin_ref[:, cols].T
            for hh in range(HEADS_PER_BLOCK):
                ones = (lane == bias_lane[hh]) | (lane == bias_lane[hh] + 1)
                kb_ref[hh, j] = jnp.where(own_lanes(lane, hh), kblk,
                                          jnp.where(ones, 1.0, 0.0)).astype(BF16)
            vt_ref[j] = vtin_ref[:, cols].astype(BF16)

    lane_q = lax.broadcasted_iota(jnp.int32, (bq, LANES), 1)
    q = q_ref[...] * (HEAD_DIM ** -0.5 * LOG2E)
    qa = []
    for hh in range(HEADS_PER_BLOCK):
        b2 = jnp.full((bq, LANES), bias_ref[HEADS_PER_BLOCK * hp + hh] * LOG2E, F32)
        b_hi = b2.astype(BF16).astype(F32)
        aug = jnp.where(lane_q == bias_lane[hh], b_hi,
                        jnp.where(lane_q == bias_lane[hh] + 1, b2 - b_hi, 0.0))
        qa.append(jnp.where(own_lanes(lane_q, hh), q, aug).astype(BF16))
    uu = uu_ref[...]

    nqs = bq // qw

    def tiles(first_block, carries, diagonal):
        order = []
        for d in reversed(range(r)):
            for hh in range(HEADS_PER_BLOCK):
                for qs in range(nqs):
                    if not diagonal or (qs + 1) * qw - 1 > d * bk:
                        order.append((hh, d, qs))
        n = len(order)
        carries = list(carries)
        zz = [None] * n
        later = [None] * n
        ptop = [None] * n

        def visible(d, qs):
            if not diagonal or qs * qw >= (d + 1) * bk:
                return None
            return (lax.broadcasted_iota(jnp.int32, (bk, qw), 0) + d * bk
                    < lax.broadcasted_iota(jnp.int32, (bk, qw), 1) + qs * qw)

        def logits(i):
            hh, d, qs = order[i]
            zz[i] = _dot_nt(kb_ref[hh, first_block + d], qa[hh][qs * qw:(qs + 1) * qw, :])

        def suffix_sums(i):
            hh, d, qs = order[i]
            p = _softplus2(zz[i])
            vis = visible(d, qs)
            if vis is not None:
                p = jnp.where(vis, p, 0.0)
            later[i] = _dot(uu, p.astype(BF16))
            ptop[i] = p[0:1, :]
            zz[i] = zz[i] - p

        def weigh(i):
            hh, d, qs = order[i]
            c = hh * nqs + qs
            w = jnp.exp2(zz[i] - later[i] - carries[c])
            vis = visible(d, qs)
            if vis is not None:
                w = jnp.where(vis, w, 0.0)
            carries[c] = carries[c] + (later[i][0:1, :] + ptop[i])
            acc_ref[hh, :, qs * qw:(qs + 1) * qw] += _dot(
                vt_ref[first_block + d, hh * HEAD_DIM:(hh + 1) * HEAD_DIM, :], w.astype(BF16))
            zz[i] = later[i] = ptop[i] = None

        for s in range(n + WEIGH_LAG):
            if s < n:
                logits(s)
            if 0 <= s - SUFFIX_LAG < n:
                suffix_sums(s - SUFFIX_LAG)
            if 0 <= s - WEIGH_LAG < n:
                weigh(s - WEIGH_LAG)
        return tuple(carries)

    acc_ref[...] = jnp.zeros_like(acc_ref)
    zero = jnp.zeros((1, qw), F32)
    carries = tiles(qi * r, (zero,) * (HEADS_PER_BLOCK * nqs), True)
    lax.fori_loop(0, qi, lambda it, c: tiles((qi - 1 - it) * r, c, False), carries)
    o_ref[...] = acc_ref[...].reshape(HEADS_PER_BLOCK * HEAD_DIM, bq).T


def _suffix_ones(bk):
    return (jnp.arange(bk)[None, :] > jnp.arange(bk)[:, None]).astype(BF16)


def _prompt_attention(q, kt, vt, bias, b, t, bq=1024, bk=128, qw=256):
    m, aw = q.shape
    nq = t // bq
    nk = t // bk
    hp = aw // LANES
    qspec = pl.BlockSpec((bq, LANES), lambda bi, h, i: (bi * nq + i, h))
    kvspec = pl.BlockSpec((LANES, t), lambda bi, h, i: (bi * hp + h, 0))
    return pl.pallas_call(
        functools.partial(_attn_kernel, bq=bq, bk=bk, qw=qw, t=t),
        grid=(b, hp, nq),
        in_specs=[pl.BlockSpec(memory_space=pltpu.SMEM), qspec, kvspec, kvspec,
                  _const_spec((bk, bk))],
        out_specs=qspec,
        out_shape=jax.ShapeDtypeStruct((m, aw), F32),
        scratch_shapes=[pltpu.VMEM((HEADS_PER_BLOCK, nk, bk, LANES), BF16),
                        pltpu.VMEM((nk, LANES, bk), BF16),
                        pltpu.VMEM((HEADS_PER_BLOCK, HEAD_DIM, bq), F32)],
        compiler_params=_cparams(3),
        name="prompt_attention",
    )(bias, q, kt, vt, _suffix_ones(bk))


def _decode_attn_kernel(pt_ref, q_ref, bias_ref, ll_ref, *refs, n_pages_step, n_heads):
    del pt_ref
    g = n_pages_step
    k_refs = refs[:g]
    v_refs = refs[g:2 * g]
    o_ref = refs[2 * g]
    acc_ref, carry_ref = refs[2 * g + 1:]
    s = pl.program_id(1)
    aw = q_ref.shape[-1]

    @pl.when(s == 0)
    def _init():
        acc_ref[...] = jnp.zeros_like(acc_ref)
        carry_ref[...] = jnp.zeros_like(carry_ref)

    head_of_lane = lax.shift_right_logical(lax.broadcasted_iota(jnp.int32, (n_heads, aw), 1),
                                           HEAD_DIM.bit_length() - 1)
    own = head_of_lane == lax.broadcasted_iota(jnp.int32, (n_heads, aw), 0)
    qrows = jnp.where(own, q_ref[0] * (HEAD_DIM ** -0.5 * LOG2E), 0.0).astype(BF16)
    bias2 = bias_ref[...] * LOG2E
    page = k_refs[0].shape[-1]
    paged = lambda ref: ref[0].reshape(aw, page).astype(BF16)
    rows = lambda a, i: a[i * n_heads:(i + 1) * n_heads]

    zz = jnp.concatenate([_dot(qrows, paged(k_refs[i])) + bias2 for i in range(g)], axis=0)
    hi, lo = _split_bf16(_softplus2(zz))
    cs = _dot(jnp.concatenate([hi, lo], axis=1), ll_ref[...])
    incl, tot = cs[:, :page], cs[:, page:]
    carry = carry_ref[...]
    carries = []
    for i in range(g):
        carries.append(carry)
        carry = carry + rows(tot, i)
    carry_ref[...] = carry
    w = jnp.exp2(zz - incl - jnp.concatenate(carries, axis=0))
    acc = acc_ref[...]
    for i in range(g):
        acc = acc + _dot_nt(rows(w, i).astype(BF16), paged(v_refs[i]))
    acc_ref[...] = acc

    @pl.when(s == pl.num_programs(1) - 1)
    def _fin():
        o_ref[0] = jnp.sum(jnp.where(own, acc_ref[...], 0.0), axis=0, keepdims=True)


def _decode_ones(page):
    sp = jnp.arange(2 * page)[:, None] % page
    s = jnp.arange(page)[None, :]
    low = (sp >= s).astype(BF16)
    return jnp.concatenate([low, jnp.ones((2 * page, page), BF16)], axis=1)


def _decode_attention(q, cache_kt, cache_vt, page_table, bias, n_pages_step=16):
    bs, aw = q.shape
    n_phys, n_heads, hd, page = cache_kt.shape
    npg = page_table.shape[1]
    g = n_pages_step
    steps = npg // g

    def page_spec(i):
        return pl.BlockSpec((1, n_heads, hd, page),
                            lambda b, s, pt: (pt[b, npg - 1 - (s * g + i)], 0, 0, 0))

    qspec = pl.BlockSpec((1, 1, aw), lambda b, s, pt: (b, 0, 0))
    grid_spec = pltpu.PrefetchScalarGridSpec(
        num_scalar_prefetch=1,
        grid=(bs, steps),
        in_specs=[qspec,
                  pl.BlockSpec((n_heads, LANES), lambda b, s, pt: (0, 0)),
                  pl.BlockSpec((2 * page, 2 * page), lambda b, s, pt: (0, 0))]
                 + [page_spec(i) for i in range(g)] * 2,
        out_specs=qspec,
        scratch_shapes=[pltpu.VMEM((n_heads, aw), F32), pltpu.VMEM((n_heads, page), F32)],
    )
    bias_b = jnp.broadcast_to(bias[:, None], (n_heads, LANES))
    out = pl.pallas_call(
        functools.partial(_decode_attn_kernel, n_pages_step=g, n_heads=n_heads),
        grid_spec=grid_spec,
        out_shape=jax.ShapeDtypeStruct((bs, 1, aw), F32),
        compiler_params=_cparams(2),
        name="decode_attention",
    )(page_table, q.reshape(bs, 1, aw), bias_b, _decode_ones(page),
      *([cache_kt] * g), *([cache_vt] * g))
    return out.reshape(bs, aw)


def _ssm_prep_kernel(lr_ref, li_ref, ldt_ref, btr_ref, bti_ref, ctr_ref, cti_ref,
                     wbr_ref, wbi_ref, wcr_ref, wci_ref, lpr_ref, lpi_ref, amr_ref, ami_ref,
                     wsr_ref, wsi_ref, *, log2_cg, log2_p):
    i = pl.program_id(0)
    blk = lr_ref.shape[-1]
    lam_r = lr_ref[...]
    lam_i = li_ref[...]
    dt = jnp.exp(ldt_ref[...])
    mag = jnp.exp(lam_r * dt)
    ang = lam_i * dt
    lb_r = mag * jnp.cos(ang)
    lb_i = mag * jnp.sin(ang)
    nr = lb_r - 1.0
    den = lam_r * lam_r + lam_i * lam_i
    cr = (nr * lam_r + lb_i * lam_i) / den
    ci = (lb_i * lam_r - nr * lam_i) / den

    shp = btr_ref.shape
    rg = lax.shift_right_logical(lax.broadcasted_iota(jnp.int32, shp, 0), log2_cg)
    cg = lax.shift_right_logical(lax.broadcasted_iota(jnp.int32, shp, 1) + i * blk, log2_p)
    same = rg == cg
    btr = btr_ref[...]
    bti = bti_ref[...]
    wbr_ref[...] = jnp.where(same, cr * btr - ci * bti, 0.0).astype(BF16)
    wbi_ref[...] = jnp.where(same, cr * bti + ci * btr, 0.0).astype(BF16)

    ch = wsr_ref.shape[1] // MXU_SCAN
    own = pl.ds(pl.multiple_of(i * ch, ch), ch)
    same = (lax.shift_right_logical(lax.broadcasted_iota(jnp.int32, (ch, blk), 0), log2_cg)
            == lax.shift_right_logical(lax.broadcasted_iota(jnp.int32, (ch, blk), 1), log2_p))
    wr = jnp.where(same, cr * btr_ref[own, :] - ci * bti_ref[own, :], 0.0)
    wi = jnp.where(same, cr * bti_ref[own, :] + ci * btr_ref[own, :], 0.0)
    for k in range(MXU_SCAN):
        wsr_ref[0, k * ch:(k + 1) * ch, :] = wr.astype(BF16)
        wsi_ref[0, k * ch:(k + 1) * ch, :] = wi.astype(BF16)
        wr, wi = wr * lb_r - wi * lb_i, wr * lb_i + wi * lb_r

    shp = ctr_ref.shape
    rg = lax.shift_right_logical(lax.broadcasted_iota(jnp.int32, shp, 0) + i * blk, log2_p)
    cg = lax.shift_right_logical(lax.broadcasted_iota(jnp.int32, shp, 1), log2_cg)
    same = rg == cg
    wcr_ref[...] = jnp.where(same, ctr_ref[...], 0.0).astype(BF16)
    wci_ref[...] = jnp.where(same, -cti_ref[...], 0.0).astype(BF16)

    rows = lax.broadcasted_iota(jnp.int32, (SUBLANES, blk), 0)
    base_r, base_i = lb_r, lb_i
    for level in range(SCAN_LEVELS):
        pr, pi = base_r, base_i
        lpr = jnp.zeros((SUBLANES, blk), F32)
        lpi = jnp.zeros((SUBLANES, blk), F32)
        slot = 0
        for n in range(1, SUBLANES + 1):
            lpr = jnp.where(rows == n - 1, pr, lpr)
            lpi = jnp.where(rows == n - 1, pi, lpi)
            if n in SCAN_SHIFTS:
                amr_ref[level, slot] = jnp.where(rows >= n, pr, 0.0)
                ami_ref[level, slot] = jnp.where(rows >= n, pi, 0.0)
                slot += 1
            if n < SUBLANES:
                pr, pi = pr * base_r - pi * base_i, pr * base_i + pi * base_r
        lpr_ref[level] = lpr
        lpi_ref[level] = lpi
        base_r, base_i = pr, pi


def _ssm_prep(lam_re, lam_im, log_dt, b_re, b_im, c_re, c_im, nblk=4):
    g, p = lam_re.shape
    cgs = b_re.shape[-1]
    gp, gc = g * p, g * cgs
    blk = gp // nblk
    flat = lambda a: a.reshape(1, gp)
    ldt = flat(jnp.broadcast_to(log_dt[:, None], (g, p)))
    bt = lambda a: jnp.tile(a.transpose(0, 2, 1).reshape(gc, p), (1, g))
    ct = lambda a: jnp.tile(a.transpose(0, 2, 1).reshape(gp, cgs), (1, g))
    lane_blk = lambda shape: pl.BlockSpec(shape, lambda i: (0,) * (len(shape) - 1) + (i,))
    return pl.pallas_call(
        functools.partial(_ssm_prep_kernel, log2_cg=int(math.log2(cgs)), log2_p=int(math.log2(p))),
        grid=(nblk,),
        in_specs=[lane_blk((1, blk))] * 3 + [lane_blk((gc, blk))] * 2
                 + [pl.BlockSpec((blk, gc), lambda i: (i, 0))] * 2,
        out_specs=[lane_blk((gc, blk))] * 2 + [pl.BlockSpec((blk, gc), lambda i: (i, 0))] * 2
                  + [lane_blk((SCAN_LEVELS, SUBLANES, blk))] * 2
                  + [lane_blk((SCAN_LEVELS, len(SCAN_SHIFTS), SUBLANES, blk))] * 2
                  + [pl.BlockSpec((1, MXU_SCAN * (gc // nblk), blk), lambda i: (i, 0, 0))] * 2,
        out_shape=[jax.ShapeDtypeStruct((gc, gp), BF16)] * 2 + [jax.ShapeDtypeStruct((gp, gc), BF16)] * 2
                  + [jax.ShapeDtypeStruct((SCAN_LEVELS, SUBLANES, gp), F32)] * 2
                  + [jax.ShapeDtypeStruct((SCAN_LEVELS, len(SCAN_SHIFTS), SUBLANES, gp), F32)] * 2
                  + [jax.ShapeDtypeStruct((nblk, MXU_SCAN * (gc // nblk), blk), BF16)] * 2,
        compiler_params=_cparams(1),
        name="ssm_prep",
    )(flat(lam_re), flat(lam_im), ldt, bt(b_re), bt(b_im), ct(c_re), ct(c_im))


def _ssm_tail(hr, hi, u, wcr_ref, wci_ref, dsk_ref, wglu_ref):
    y = _dot(hr.astype(BF16), wcr_ref[...]) + _dot(hi.astype(BF16), wci_ref[...]) + dsk_ref[...] * u
    zs = _gelu(y)
    return zs * _sigmoid(_dot(zs.astype(BF16), wglu_ref[...]))


def _ssm_scan_kernel(u_ref, wsr_ref, wsi_ref, wcr_ref, wci_ref, lpr_ref, lpi_ref, amr_ref, ami_ref,
                     dsk_ref, wglu_ref, o_ref, hr_ref, hi_ref, xr_ref, xi_ref, cr_ref, ci_ref,
                     sr_ref, si_ref):
    tc = pl.program_id(1)
    lt = u_ref.shape[0]
    gp = lpr_ref.shape[-1]
    ng = lt // SUBLANES
    nblk, _, blk = wsr_ref.shape
    ch = u_ref.shape[1] // nblk

    @pl.when(tc == 0)
    def _init():
        cr_ref[...] = jnp.zeros_like(cr_ref)
        ci_ref[...] = jnp.zeros_like(ci_ref)

    u = u_ref[...]
    row = lax.broadcasted_iota(jnp.int32, (ng, SUBLANES, ch), 1)
    parts_r, parts_i = [], []
    for c in range(nblk):
        uc = u[:, c * ch:(c + 1) * ch].reshape(ng, SUBLANES, ch)
        lagged = [uc] + [jnp.where(row >= k, pltpu.roll(uc, k, axis=1), 0.0) for k in range(1, MXU_SCAN)]
        lhs = jnp.concatenate(lagged, axis=2).reshape(lt, MXU_SCAN * ch).astype(BF16)
        parts_r.append(_dot(lhs, wsr_ref[c]))
        parts_i.append(_dot(lhs, wsi_ref[c]))
    xr = jnp.concatenate(parts_r, axis=1).reshape(ng, SUBLANES, gp)
    xi = jnp.concatenate(parts_i, axis=1).reshape(ng, SUBLANES, gp)
    def scan8(ar3, ai3, level, first_shift):
        for idx, d in enumerate(SCAN_SHIFTS):
            if d < first_shift:
                continue
            mr = amr_ref[level, idx]
            mi = ami_ref[level, idx]
            sr = pltpu.roll(ar3, d, axis=1)
            si = pltpu.roll(ai3, d, axis=1)
            ar3, ai3 = ar3 + mr * sr - mi * si, ai3 + mr * si + mi * sr
        return ar3, ai3

    xr, xi = scan8(xr, xi, 0, MXU_SCAN)
    nl = gp // LANES
    tile = lambda a, l: a[:, l * LANES:(l + 1) * LANES]
    whole = lambda ref, rows: jnp.concatenate([ref[l, rows, :] for l in range(nl)], axis=1)
    xr, xi = xr.reshape(lt, gp), xi.reshape(lt, gp)
    for l in range(nl):
        xr_ref[l] = tile(xr, l)
        xi_ref[l] = tile(xi, l)

    last = SUBLANES - 1
    nsg = ng // SUBLANES
    ends = pl.ds(last, ng, stride=SUBLANES)
    er, ei = scan8(whole(xr_ref, ends).reshape(nsg, SUBLANES, gp),
                   whole(xi_ref, ends).reshape(nsg, SUBLANES, gp), 1, 1)
    l1r, l1i = lpr_ref[1], lpi_ref[1]
    hr, hi = cr_ref[...], ci_ref[...]
    sr_ref[last:SUBLANES, :] = hr[0:1, :]
    si_ref[last:SUBLANES, :] = hi[0:1, :]
    for j in range(nsg):
        gr = er[j] + l1r * hr - l1i * hi
        gi = ei[j] + l1r * hi + l1i * hr
        sr_ref[SUBLANES * (j + 1):SUBLANES * (j + 2), :] = gr
        si_ref[SUBLANES * (j + 1):SUBLANES * (j + 2), :] = gi
        hr = jnp.broadcast_to(gr[last:, :], gr.shape)
        hi = jnp.broadcast_to(gi[last:, :], gi.shape)
    cr_ref[...] = hr
    ci_ref[...] = hi
    hr_ref[0] = hr[0:1, :]
    hi_ref[0] = hi[0:1, :]

    l0r, l0i = lpr_ref[0], lpi_ref[0]
    for i in range(ng):
        rows = slice(i * SUBLANES, (i + 1) * SUBLANES)
        pr = jnp.broadcast_to(sr_ref[last + i:last + i + 1, :], (SUBLANES, gp))
        pi = jnp.broadcast_to(si_ref[last + i:last + i + 1, :], (SUBLANES, gp))
        dr = l0r * pr - l0i * pi
        di = l0r * pi + l0i * pr
        for l in range(nl):
            xr_ref[l, rows, :] += tile(dr, l)
            xi_ref[l, rows, :] += tile(di, l)
    everything = slice(None)
    o_ref[...] = _ssm_tail(whole(xr_ref, everything), whole(xi_ref, everything), u,
                           wcr_ref, wci_ref, dsk_ref, wglu_ref)


def _ssm_prompt(u, prep, dsk, wglu_bf, b, t, lt=256):
    _, _, wcr, wci, lpr, lpi, amr, ami, wsr, wsi = prep
    m, sw = u.shape
    gp = lpr.shape[-1]
    nt = t // lt
    row = pl.BlockSpec((lt, sw), lambda bi, i: (bi * nt + i, 0))
    hspec = pl.BlockSpec((1, 1, gp), lambda bi, i: (bi, 0, 0))
    return pl.pallas_call(
        _ssm_scan_kernel,
        grid=(b, nt),
        in_specs=[row, _const_spec(wsr.shape), _const_spec(wsi.shape), _const_spec((gp, sw)),
                  _const_spec((gp, sw)), _const_spec(lpr.shape), _const_spec(lpi.shape),
                  _const_spec(amr.shape), _const_spec(ami.shape),
                  _const_spec((1, sw)), _const_spec((sw, sw))],
        out_specs=[row, hspec, hspec],
        out_shape=[jax.ShapeDtypeStruct((m, sw), F32), jax.ShapeDtypeStruct((b, 1, gp), F32),
                   jax.ShapeDtypeStruct((b, 1, gp), F32)],
        scratch_shapes=[pltpu.VMEM((gp // LANES, lt, LANES), F32)] * 2 + [pltpu.VMEM((SUBLANES, gp), F32)] * 2
                       + [pltpu.VMEM((SUBLANES + lt // SUBLANES, gp), F32)] * 2,
        compiler_params=_cparams(2),
        name="ssm_scan",
    )(u, wsr, wsi, wcr, wci, lpr, lpi, amr, ami, dsk, wglu_bf)


def _ssm_step_kernel(u_ref, h0r_ref, h0i_ref, wbr_ref, wbi_ref, wcr_ref, wci_ref, lpr_ref, lpi_ref,
                     dsk_ref, wglu_ref, o_ref, hr_ref, hi_ref):
    u = u_ref[...]
    ub = u.astype(BF16)
    lr = lpr_ref[0, 0:1, :]
    li = lpi_ref[0, 0:1, :]
    h0r = h0r_ref[...]
    h0i = h0i_ref[...]
    hr = _dot(ub, wbr_ref[...]) + (lr * h0r - li * h0i)
    hi = _dot(ub, wbi_ref[...]) + (lr * h0i + li * h0r)
    hr_ref[...] = hr
    hi_ref[...] = hi
    o_ref[...] = _ssm_tail(hr, hi, u, wcr_ref, wci_ref, dsk_ref, wglu_ref)


def _ssm_step(u, h0r, h0i, prep, dsk, wglu_bf):
    wbr, wbi, wcr, wci, lpr, lpi = prep[:6]
    m, sw = u.shape
    gp = wbr.shape[1]
    return pl.pallas_call(
        _ssm_step_kernel,
        out_shape=[jax.ShapeDtypeStruct((m, sw), F32), jax.ShapeDtypeStruct((m, gp), F32),
                   jax.ShapeDtypeStruct((m, gp), F32)],
        compiler_params=pltpu.CompilerParams(vmem_limit_bytes=VMEM_LIMIT),
        name="ssm_step",
    )(u, h0r, h0i, wbr, wbi, wcr, wci, lpr, lpi, dsk, wglu_bf)


def _mix_residual(x, att, ssm, ga_ref, gs_ref, wo_ref, gpm_ref):
    an = _rms(att, ga_ref[...]).astype(BF16)
    sn = _rms(ssm, gs_ref[...]).astype(BF16)
    aw = an.shape[-1]
    y = _dot(an, wo_ref[0:aw, :]) + _dot(sn, wo_ref[aw:, :])
    return x + _rms(y, gpm_ref[...])


def _gated_mlp(xn, wg_ref, wu_ref, wd_ref, conv, fc):
    f = wg_ref.shape[1]
    acc = None
    pending = None
    for cs in [slice(c * fc, (c + 1) * fc) for c in range(f // fc)]:
        gate = _dot(xn, wg_ref[:, cs])
        up = _dot(xn, wu_ref[:, cs])
        if pending is not None:
            part = _dot(pending[0], wd_ref[pending[1], :])
            acc = part if acc is None else acc + part
        pending = ((_gelu(conv(gate, cs)) * up).astype(BF16), cs)
    part = _dot(pending[0], wd_ref[pending[1], :])
    return part if acc is None else acc + part


def _tail_seq_kernel(x_ref, att_ref, ssm_ref, ga_ref, gs_ref, wo_ref, gpm_ref,
                     g_ref, wg_ref, wu_ref, cw_ref, cb_ref, wd_ref, gpost_ref,
                     o_ref, cn_ref, gate_ref, *, fc, blocks_per_seq):
    i = pl.program_id(0)
    bm = x_ref.shape[0]
    hist = SUBLANES
    first = (i % blocks_per_seq) == 0

    @pl.when(first)
    def _zero_hist():
        gate_ref[0:hist, :] = jnp.zeros((hist, gate_ref.shape[1]), F32)

    @pl.when(jnp.logical_not(first))
    def _keep_hist():
        gate_ref[0:hist, :] = gate_ref[bm:bm + hist, :]

    x1 = _mix_residual(x_ref[...], att_ref[...], ssm_ref[...], ga_ref, gs_ref, wo_ref, gpm_ref)
    xn = _rms(x1, g_ref[...]).astype(BF16)

    def conv(gate, cs):
        gate_ref[hist:hist + bm, cs] = gate
        g1 = gate_ref[hist - 1:hist - 1 + bm, cs]
        g2 = gate_ref[hist - 2:hist - 2 + bm, cs]
        return cb_ref[:, cs] + g2 * cw_ref[0:1, cs] + g1 * cw_ref[1:2, cs] + gate * cw_ref[2:3, cs]

    acc = _gated_mlp(xn, wg_ref, wu_ref, wd_ref, conv, fc)
    cn_ref[0] = gate_ref[hist + bm - 2:hist + bm, :]
    o_ref[...] = x1 + _rms(acc, gpost_ref[...])


def _resident(shape):
    return pl.BlockSpec(shape, lambda *_: (0,) * len(shape), pipeline_mode=pl.Buffered(1))


def _tail_weight_specs(d, aw, sw, f):
    return [_const_spec((1, aw)), _const_spec((1, sw)), _resident((aw + sw, d)), _const_spec((1, d)),
            _const_spec((1, d)), _resident((d, f)), _resident((d, f)), _const_spec((3, f)),
            _const_spec((1, f)), _resident((f, d)), _const_spec((1, d))]


def _tail_seq(x, att, ssm, weights, b, t, bm=256, fc=256):
    m, d = x.shape
    aw, sw = att.shape[1], ssm.shape[1]
    f = weights[5].shape[1]
    bps = t // bm
    row = lambda w: pl.BlockSpec((bm, w), lambda i: (i, 0))
    return pl.pallas_call(
        functools.partial(_tail_seq_kernel, fc=fc, blocks_per_seq=bps),
        grid=(m // bm,),
        in_specs=[row(d), row(aw), row(sw)] + _tail_weight_specs(d, aw, sw, f),
        out_specs=[row(d), pl.BlockSpec((1, 2, f), lambda i: (i // bps, 0, 0))],
        out_shape=[jax.ShapeDtypeStruct((m, d), F32), jax.ShapeDtypeStruct((b, 2, f), F32)],
        scratch_shapes=[pltpu.VMEM((SUBLANES + bm, f), F32)],
        compiler_params=_cparams(1),
        name="tail_seq",
    )(x, att, ssm, *weights)


def _tail_step_kernel(x_ref, att_ref, ssm_ref, ga_ref, gs_ref, wo_ref, gpm_ref,
                      g_ref, wg_ref, wu_ref, cw_ref, cb_ref, wd_ref, gpost_ref, p0_ref, p1_ref,
                      o_ref, gate_ref, *, fc):
    x1 = _mix_residual(x_ref[...], att_ref[...], ssm_ref[...], ga_ref, gs_ref, wo_ref, gpm_ref)
    xn = _rms(x1, g_ref[...]).astype(BF16)

    def conv(gate, cs):
        gate_ref[:, cs] = gate
        return (cb_ref[:, cs] + p0_ref[:, cs] * cw_ref[0:1, cs] + p1_ref[:, cs] * cw_ref[1:2, cs]
                + gate * cw_ref[2:3, cs])

    acc = _gated_mlp(xn, wg_ref, wu_ref, wd_ref, conv, fc)
    o_ref[...] = x1 + _rms(acc, gpost_ref[...])


def _tail_step(x, att, ssm, weights, p0, p1, fc=256):
    m, d = x.shape
    f = weights[5].shape[1]
    return pl.pallas_call(
        functools.partial(_tail_step_kernel, fc=fc),
        out_shape=[jax.ShapeDtypeStruct((m, d), F32), jax.ShapeDtypeStruct((m, f), F32)],
        compiler_params=pltpu.CompilerParams(vmem_limit_bytes=VMEM_LIMIT),
        name="tail_step",
    )(x, att, ssm, *weights, p0, p1)


def kernel(x_prompt, x_sample, cache_k, cache_v, state_ssm_re, state_ssm_im, state_ffn_conv, page_table,
           g_pre_mix, w_in, sb_bias, g_att_out, lam_re, lam_im, log_dt, b_re, b_im, c_re, c_im, d_skip,
           w_glu, g_ssm_out, w_out, g_post_mix, g_pre_ffn, w_gate, w_up, conv_w, conv_b, w_down,
           g_post_ffn):
    depth = w_in.shape[0]
    assert depth == 1, "single-layer step"
    b, t, d = x_prompt.shape
    bs, ts, _ = x_sample.shape
    assert ts == 1, "sample group advances one token per sequence"
    n_heads, hd = cache_k.shape[-2:]
    assert hd == HEAD_DIM
    aw = n_heads * hd
    g, p = lam_re.shape[1:]
    l = 0

    row = lambda a: a[l].reshape(1, -1)
    w_in_bf = w_in[l].astype(BF16)
    wglu_bf = w_glu[l].astype(BF16)
    wo_bf = w_out[l].astype(BF16)
    wg_bf = w_gate[l].astype(BF16)
    wu_bf = w_up[l].astype(BF16)
    wd_bf = w_down[l].astype(BF16)
    bias = sb_bias[l]
    dsk = row(d_skip)
    tail_w = (row(g_att_out), row(g_ssm_out), wo_bf, row(g_post_mix), row(g_pre_ffn), wg_bf, wu_bf,
              conv_w[l], row(conv_b), wd_bf, row(g_post_ffn))
    prep = _ssm_prep(lam_re[l], lam_im[l], log_dt[l], b_re[l], b_im[l], c_re[l], c_im[l])

    xp = x_prompt.reshape(b * t, d)
    qp, ktp, vtp, up = _inproj_seq(xp, row(g_pre_mix), w_in_bf, aw, b, t, bm=512)
    att_p = _prompt_attention(qp, ktp, vtp, bias, b, t)
    ssm_p, hrp, hip = _ssm_prompt(up, prep, dsk, wglu_bf, b, t)
    yp, conv_p = _tail_seq(xp, att_p, ssm_p, tail_w, b, t)

    xs = x_sample.reshape(bs, d)
    qs, ks, vs, us = _inproj(xs, row(g_pre_mix), w_in_bf, aw, bm=bs)
    att_s = _decode_attention(qs, cache_k[l].transpose(0, 2, 3, 1), cache_v[l].transpose(0, 2, 3, 1),
                              page_table, bias)
    ssm_s, hrs, his = _ssm_step(us, state_ssm_re[l].reshape(bs, g * p), state_ssm_im[l].reshape(bs, g * p),
                                prep, dsk, wglu_bf)
    prev = state_ffn_conv[l]
    ys, gate_s = _tail_step(xs, att_s, ssm_s, tail_w, prev[:, 0], prev[:, 1])
    conv_s = jnp.stack([prev[:, 1], gate_s], axis=1)

    heads = lambda a, n, s: a.reshape(1, n, s, n_heads, hd)
    heads_t = lambda a: a.reshape(1, b, n_heads, hd, t).transpose(0, 1, 4, 2, 3)
    state = lambda a, n: a.reshape(1, n, g, p)
    return (yp.reshape(b, t, d), ys.reshape(bs, 1, d),
            heads_t(ktp), heads_t(vtp), heads(ks, bs, 1), heads(vs, bs, 1),
            state(hrp, b), state(hip, b), state(hrs, bs), state(his, bs),
            conv_p[None], conv_s[None])
```

```python
import functools
import math

import jax
import jax.numpy as jnp
from jax import lax
from jax.experimental import pallas as pl
from jax.experimental.pallas import tpu as pltpu

F32 = jnp.float32
BF16 = jnp.bfloat16

RMS_EPS = 1e-6
HEAD_DIM = 64
LANES = 128
SUBLANES = 8
HEADS_PER_BLOCK = LANES // HEAD_DIM
SQRT_2_OVER_PI = math.sqrt(2.0 / math.pi)
LOG2E = math.log2(math.e)
SUFFIX_LAG = 3
WEIGH_LAG = 7
MXU_SCAN = 4
SCAN_SHIFTS = (1, 2, 4)
SCAN_LEVELS = 2
DECODE_GROUP = 16
DECODE_SPREAD = 16
V7X_VMEM_BYTES = 64 * 1024 * 1024
VMEM_LIMIT = V7X_VMEM_BYTES - 8 * 1024 * 1024


def _cparams(n_axes):
    return pltpu.CompilerParams(dimension_semantics=("arbitrary",) * n_axes,
                                vmem_limit_bytes=VMEM_LIMIT)


def _rms(x, g):
    return x * lax.rsqrt(jnp.mean(x * x, axis=-1, keepdims=True) + RMS_EPS) * g


def _gelu(x):
    return 0.5 * x * (1.0 + jnp.tanh(SQRT_2_OVER_PI * (x + 0.044715 * (x * x * x))))


def _sigmoid(x):
    return 1.0 / (1.0 + jnp.exp(-x))


def _softplus2(zz):
    return jnp.maximum(zz, 0.0) + jnp.log(1.0 + jnp.exp2(-jnp.abs(zz))) * LOG2E


def _split_bf16(x):
    hi = x.astype(BF16)
    return hi, (x - hi.astype(F32)).astype(BF16)


def _dot(a, b):
    return jnp.dot(a, b, preferred_element_type=F32)


def _dot_nt(a, b):
    return lax.dot_general(a, b, (((1,), (1,)), ((), ())), preferred_element_type=F32)


def _const_spec(shape):
    return pl.BlockSpec(shape, lambda *_: (0,) * len(shape))


def _inproj_kernel(x_ref, g_ref, w_ref, q_ref, k_ref, v_ref, u_ref):
    xn = _rms(x_ref[...], g_ref[...]).astype(BF16)
    aw = q_ref.shape[-1]
    q_ref[...] = _dot(xn, w_ref[:, 0:aw])
    k_ref[...] = _dot(xn, w_ref[:, aw:2 * aw])
    v_ref[...] = _dot(xn, w_ref[:, 2 * aw:3 * aw])
    u_ref[...] = _dot(xn, w_ref[:, 3 * aw:])


def _inproj(x, g, w_bf, att_w, bm):
    m, d = x.shape
    n = w_bf.shape[1]
    sw = n - 3 * att_w
    row = lambda i: (i, 0)
    return pl.pallas_call(
        _inproj_kernel,
        grid=(m // bm,),
        in_specs=[pl.BlockSpec((bm, d), row), _const_spec((1, d)), _const_spec((d, n))],
        out_specs=[pl.BlockSpec((bm, att_w), row)] * 3 + [pl.BlockSpec((bm, sw), row)],
        out_shape=[jax.ShapeDtypeStruct((m, att_w), F32)] * 3 + [jax.ShapeDtypeStruct((m, sw), F32)],
        compiler_params=_cparams(1),
        name="inproj",
    )(x, g, w_bf)


def _inproj_seq_kernel(x_ref, g_ref, wq_ref, wkt_ref, wvt_ref, wu_ref, q_ref, kt_ref, vt_ref, u_ref):
    xn = _rms(x_ref[...], g_ref[...]).astype(BF16)
    q_ref[...] = _dot(xn, wq_ref[...])
    kt_ref[...] = _dot_nt(wkt_ref[...], xn)
    vt_ref[...] = _dot_nt(wvt_ref[...], xn)
    u_ref[...] = _dot(xn, wu_ref[...])


def _inproj_seq(x, g, w_bf, att_w, b, t, bm):
    m, d = x.shape
    n = w_bf.shape[1]
    sw = n - 3 * att_w
    nb = t // bm
    row = lambda i: (i, 0)
    tspec = pl.BlockSpec((att_w, bm), lambda i: (i // nb, i % nb))
    wq, wk, wv, wu = (w_bf[:, 0:att_w], w_bf[:, att_w:2 * att_w], w_bf[:, 2 * att_w:3 * att_w],
                      w_bf[:, 3 * att_w:])
    return pl.pallas_call(
        _inproj_seq_kernel,
        grid=(m // bm,),
        in_specs=[pl.BlockSpec((bm, d), row), _const_spec((1, d)), _const_spec((d, att_w)),
                  _const_spec((att_w, d)), _const_spec((att_w, d)), _const_spec((d, sw))],
        out_specs=[pl.BlockSpec((bm, att_w), row), tspec, tspec, pl.BlockSpec((bm, sw), row)],
        out_shape=[jax.ShapeDtypeStruct((m, att_w), F32), jax.ShapeDtypeStruct((b * att_w, t), F32),
                   jax.ShapeDtypeStruct((b * att_w, t), F32), jax.ShapeDtypeStruct((m, sw), F32)],
        compiler_params=_cparams(1),
        name="inproj_seq",
    )(x, g, wq, wk.T, wv.T, wu)


def _decode_stages(step, pt_ref, qs_ref, biasb_ref, ll_ref, ck_ref, cv_ref, os_ref,
                   kpg_ref, vpg_ref, sem, dacc_ref, dcarry_ref, *, n_steps):
    bs, npg = pt_ref.shape
    _, pps, n_heads, hd, page = kpg_ref.shape
    aw = n_heads * hd
    sps = n_steps // bs
    grp = step % sps
    slot = step % 2

    def page_copies(s, slot_):
        seq = s // sps
        first_page = npg - 1 - (s % sps) * pps
        for i in range(pps):
            pid = pt_ref[seq, first_page - i]
            yield pltpu.make_async_copy(ck_ref.at[pid], kpg_ref.at[slot_, i], sem.at[slot_, 0])
            yield pltpu.make_async_copy(cv_ref.at[pid], vpg_ref.at[slot_, i], sem.at[slot_, 1])

    @pl.when(step == 0)
    def _fetch_first():
        for cp in page_copies(step, slot):
            cp.start()

    @pl.when(step + 1 < n_steps)
    def _fetch_next():
        for cp in page_copies(step + 1, 1 - slot):
            cp.start()

    for cp in page_copies(step, slot):
        cp.wait()

    @pl.when(grp == 0)
    def _new_sequence():
        dacc_ref[...] = jnp.zeros_like(dacc_ref)
        dcarry_ref[...] = jnp.zeros_like(dcarry_ref)

    head_of_lane = lax.shift_right_logical(lax.broadcasted_iota(jnp.int32, (n_heads, aw), 1),
                                           HEAD_DIM.bit_length() - 1)
    own = head_of_lane == lax.broadcasted_iota(jnp.int32, (n_heads, aw), 0)
    qrows = jnp.where(own, qs_ref[0] * (HEAD_DIM ** -0.5 * LOG2E), 0.0).astype(BF16)
    bias2 = biasb_ref[...] * LOG2E
    paged = lambda ref, i: ref[slot, i].reshape(aw, page).astype(BF16)
    rows = lambda a, i: a[i * n_heads:(i + 1) * n_heads]

    gsz = min(DECODE_GROUP, pps)
    n_groups = pps // gsz
    per_step = -(-pps // DECODE_SPREAD)
    zz = [None] * pps
    sums = [None] * n_groups
    weights = [None] * n_groups
    state = {"acc": None, "carry": None}

    def logits(i):
        zz[i] = _dot(qrows, paged(kpg_ref, i)) + bias2

    def suffix_sums(g):
        z = jnp.concatenate(zz[g * gsz:(g + 1) * gsz], axis=0)
        p = _softplus2(z)
        hi, lo = _split_bf16(p)
        cs = _dot(jnp.concatenate([hi, lo], axis=1), ll_ref[...])
        sums[g] = (z - p, cs[:, :page], cs[:, page:])

    def weigh(g):
        if state["carry"] is None:
            state["carry"] = dcarry_ref[...]
            state["acc"] = dacc_ref[...]
        logsig, later, tot = sums[g]
        carries = []
        for i in range(gsz):
            carries.append(state["carry"])
            state["carry"] = state["carry"] + rows(tot, i)
        weights[g] = jnp.exp2(logsig - later - jnp.concatenate(carries, axis=0))

    def values(i):
        g, j = divmod(i, gsz)
        state["acc"] = state["acc"] + _dot_nt(rows(weights[g], j).astype(BF16), paged(vpg_ref, i))

    def finish():
        dacc_ref[...] = state["acc"]
        dcarry_ref[...] = state["carry"]
        os_ref[0] = jnp.sum(jnp.where(own, state["acc"], 0.0), axis=0, keepdims=True)

    plan = {}
    at = lambda s, f, *a: plan.setdefault(s, []).append(functools.partial(f, *a))
    ready = 0
    for g in range(n_groups):
        for i in range(g * gsz, (g + 1) * gsz):
            at(i // per_step, logits, i)
        s_sum = max(ready, ((g + 1) * gsz - 1) // per_step + SUFFIX_LAG + 1)
        at(s_sum, suffix_sums, g)
        s_w = s_sum + SUFFIX_LAG + 1
        at(s_w, weigh, g)
        for j in range(gsz):
            at(s_w + j // per_step, values, g * gsz + j)
        ready = s_w + (gsz - 1) // per_step + 1
    at(ready, finish)
    return plan


def _attn_kernel(pt_ref, bias_ref, q_ref, ktin_ref, vtin_ref, uu_ref, qs_ref, biasb_ref, ll_ref,
                 ck_ref, cv_ref, o_ref, os_ref, kb_ref, vt_ref, acc_ref, kpg_ref, vpg_ref, sem,
                 dacc_ref, dcarry_ref, *, bq, bk, qw, t, n_steps):
    hp = pl.program_id(1)
    qi = pl.program_id(2)
    nk = t // bk
    r = bq // bk
    step = (pl.program_id(0) * pl.num_programs(1) + hp) * pl.num_programs(2) + qi
    decode_plan = _decode_stages(step, pt_ref, qs_ref, biasb_ref, ll_ref, ck_ref, cv_ref, os_ref,
                                 kpg_ref, vpg_ref, sem, dacc_ref, dcarry_ref, n_steps=n_steps)
    bias_lane = [(1 - hh) * HEAD_DIM for hh in range(HEADS_PER_BLOCK)]

    def own_lanes(lane, hh):
        return (lane >= hh * HEAD_DIM) & (lane < (hh + 1) * HEAD_DIM)

    @pl.when(qi == 0)
    def _prep():
        lane = lax.broadcasted_iota(jnp.int32, (bk, LANES), 1)
        for j in range(nk):
            cols = slice(j * bk, (j + 1) * bk)
            kblk = ktin_ref[:, cols].T
            for hh in range(HEADS_PER_BLOCK):
                ones = (lane == bias_lane[hh]) | (lane == bias_lane[hh] + 1)
                kb_ref[hh, j] = jnp.where(own_lanes(lane, hh), kblk,
                                          jnp.where(ones, 1.0, 0.0)).astype(BF16)
            vt_ref[j] = vtin_ref[:, cols].astype(BF16)

    lane_q = lax.broadcasted_iota(jnp.int32, (bq, LANES), 1)
    q = q_ref[...] * (HEAD_DIM ** -0.5 * LOG2E)
    qa = []
    for hh in range(HEADS_PER_BLOCK):
        b2 = jnp.full((bq, LANES), bias_ref[HEADS_PER_BLOCK * hp + hh] * LOG2E, F32)
        b_hi = b2.astype(BF16).astype(F32)
        aug = jnp.where(lane_q == bias_lane[hh], b_hi,
                        jnp.where(lane_q == bias_lane[hh] + 1, b2 - b_hi, 0.0))
        qa.append(jnp.where(own_lanes(lane_q, hh), q, aug).astype(BF16))
    uu = uu_ref[...]

    nqs = bq // qw

    def tiles(first_block, carries, diagonal, riders):
        order = []
        for d in reversed(range(r)):
            for hh in range(HEADS_PER_BLOCK):
                for qs in range(nqs):
                    if not diagonal or (qs + 1) * qw - 1 > d * bk:
                        order.append((hh, d, qs))
        n = len(order)
        carries = list(carries)
        zz = [None] * n
        later = [None] * n
        ptop = [None] * n

        def visible(d, qs):
            if not diagonal or qs * qw >= (d + 1) * bk:
                return None
            return (lax.broadcasted_iota(jnp.int32, (bk, qw), 0) + d * bk
                    < lax.broadcasted_iota(jnp.int32, (bk, qw), 1) + qs * qw)

        def logits(i):
            hh, d, qs = order[i]
            zz[i] = _dot_nt(kb_ref[hh, first_block + d], qa[hh][qs * qw:(qs + 1) * qw, :])

        def suffix_sums(i):
            hh, d, qs = order[i]
            p = _softplus2(zz[i])
            vis = visible(d, qs)
            if vis is not None:
                p = jnp.where(vis, p, 0.0)
            later[i] = _dot(uu, p.astype(BF16))
            ptop[i] = p[0:1, :]
            zz[i] = zz[i] - p

        def weigh(i):
            hh, d, qs = order[i]
            c = hh * nqs + qs
            w = jnp.exp2(zz[i] - later[i] - carries[c])
            vis = visible(d, qs)
            if vis is not None:
                w = jnp.where(vis, w, 0.0)
            carries[c] = carries[c] + (later[i][0:1, :] + ptop[i])
            acc_ref[hh, :, qs * qw:(qs + 1) * qw] += _dot(
                vt_ref[first_block + d, hh * HEAD_DIM:(hh + 1) * HEAD_DIM, :], w.astype(BF16))
            zz[i] = later[i] = ptop[i] = None

        for s in range(max(n + WEIGH_LAG, max(riders, default=-1) + 1)):
            if s < n:
                logits(s)
            if 0 <= s - SUFFIX_LAG < n:
                suffix_sums(s - SUFFIX_LAG)
            if 0 <= s - WEIGH_LAG < n:
                weigh(s - WEIGH_LAG)
            for extra in riders.get(s, ()):
                extra()
        return tuple(carries)

    acc_ref[...] = jnp.zeros_like(acc_ref)
    zero = jnp.zeros((1, qw), F32)
    carries = tiles(qi * r, (zero,) * (HEADS_PER_BLOCK * nqs), True, decode_plan)
    lax.fori_loop(0, qi, lambda it, c: tiles((qi - 1 - it) * r, c, False, {}), carries)
    o_ref[...] = acc_ref[...].reshape(HEADS_PER_BLOCK * HEAD_DIM, bq).T


def _suffix_ones(bk):
    return (jnp.arange(bk)[None, :] > jnp.arange(bk)[:, None]).astype(BF16)


def _later_ones(page):
    sp = jnp.arange(2 * page)[:, None] % page
    s = jnp.arange(page)[None, :]
    low = (sp > s).astype(BF16)
    return jnp.concatenate([low, jnp.ones((2 * page, page), BF16)], axis=1)


def _attention(q, kt, vt, qs, cache_kt, cache_vt, page_table, bias, b, t, bq=1024, bk=128, qw=256):
    m, aw = q.shape
    bs = qs.shape[0]
    n_phys, n_heads, hd, page = cache_kt.shape
    npg = page_table.shape[1]
    nq = t // bq
    nk = t // bk
    hp = aw // LANES
    n_steps = b * hp * nq
    sps = n_steps // bs
    assert sps * bs == n_steps and npg % sps == 0, "sample sequences must tile the prompt grid"
    pps = npg // sps
    seq_of = lambda bi, h, i: ((bi * hp + h) * nq + i) // sps
    qspec = pl.BlockSpec((bq, LANES), lambda bi, h, i, pt: (bi * nq + i, h))
    kvspec = pl.BlockSpec((LANES, t), lambda bi, h, i, pt: (bi * hp + h, 0))
    sspec = pl.BlockSpec((1, 1, aw), lambda bi, h, i, pt: (seq_of(bi, h, i), 0, 0))
    const = lambda shape: pl.BlockSpec(shape, lambda bi, h, i, pt: (0,) * len(shape))
    grid_spec = pltpu.PrefetchScalarGridSpec(
        num_scalar_prefetch=1,
        grid=(b, hp, nq),
        in_specs=[pl.BlockSpec(memory_space=pltpu.SMEM), qspec, kvspec, kvspec, const((bk, bk)),
                  sspec, const((n_heads, LANES)), const((2 * page, 2 * page)),
                  pl.BlockSpec(memory_space=pl.ANY), pl.BlockSpec(memory_space=pl.ANY)],
        out_specs=[qspec, sspec],
        scratch_shapes=[pltpu.VMEM((HEADS_PER_BLOCK, nk, bk, LANES), BF16),
                        pltpu.VMEM((nk, LANES, bk), BF16),
                        pltpu.VMEM((HEADS_PER_BLOCK, HEAD_DIM, bq), F32),
                        pltpu.VMEM((2, pps, n_heads, hd, page), F32),
                        pltpu.VMEM((2, pps, n_heads, hd, page), F32),
                        pltpu.SemaphoreType.DMA((2, 2)),
                        pltpu.VMEM((n_heads, aw), F32),
                        pltpu.VMEM((n_heads, page), F32)],
    )
    att, att_s = pl.pallas_call(
        functools.partial(_attn_kernel, bq=bq, bk=bk, qw=qw, t=t, n_steps=n_steps),
        grid_spec=grid_spec,
        out_shape=[jax.ShapeDtypeStruct((m, aw), F32), jax.ShapeDtypeStruct((bs, 1, aw), F32)],
        compiler_params=_cparams(3),
        name="attention",
    )(page_table, bias, q, kt, vt, _suffix_ones(bk), qs.reshape(bs, 1, aw),
      jnp.broadcast_to(bias[:, None], (n_heads, LANES)), _later_ones(page), cache_kt, cache_vt)
    return att, att_s.reshape(bs, aw)


def _ssm_prep_kernel(lr_ref, li_ref, ldt_ref, btr_ref, bti_ref, ctr_ref, cti_ref,
                     wbr_ref, wbi_ref, wcr_ref, wci_ref, lpr_ref, lpi_ref, amr_ref, ami_ref,
                     wsr_ref, wsi_ref, *, log2_cg, log2_p):
    i = pl.program_id(0)
    blk = lr_ref.shape[-1]
    lam_r = lr_ref[...]
    lam_i = li_ref[...]
    dt = jnp.exp(ldt_ref[...])
    mag = jnp.exp(lam_r * dt)
    ang = lam_i * dt
    lb_r = mag * jnp.cos(ang)
    lb_i = mag * jnp.sin(ang)
    nr = lb_r - 1.0
    den = lam_r * lam_r + lam_i * lam_i
    cr = (nr * lam_r + lb_i * lam_i) / den
    ci = (lb_i * lam_r - nr * lam_i) / den

    shp = btr_ref.shape
    rg = lax.shift_right_logical(lax.broadcasted_iota(jnp.int32, shp, 0), log2_cg)
    cg = lax.shift_right_logical(lax.broadcasted_iota(jnp.int32, shp, 1) + i * blk, log2_p)
    same = rg == cg
    btr = btr_ref[...]
    bti = bti_ref[...]
    wbr_ref[...] = jnp.where(same, cr * btr - ci * bti, 0.0).astype(BF16)
    wbi_ref[...] = jnp.where(same, cr * bti + ci * btr, 0.0).astype(BF16)

    ch = wsr_ref.shape[1] // MXU_SCAN
    own = pl.ds(pl.multiple_of(i * ch, ch), ch)
    same = (lax.shift_right_logical(lax.broadcasted_iota(jnp.int32, (ch, blk), 0), log2_cg)
            == lax.shift_right_logical(lax.broadcasted_iota(jnp.int32, (ch, blk), 1), log2_p))
    wr = jnp.where(same, cr * btr_ref[own, :] - ci * bti_ref[own, :], 0.0)
    wi = jnp.where(same, cr * bti_ref[own, :] + ci * btr_ref[own, :], 0.0)
    for k in range(MXU_SCAN):
        wsr_ref[0, k * ch:(k + 1) * ch, :] = wr.astype(BF16)
        wsi_ref[0, k * ch:(k + 1) * ch, :] = wi.astype(BF16)
        wr, wi = wr * lb_r - wi * lb_i, wr * lb_i + wi * lb_r

    shp = ctr_ref.shape
    rg = lax.shift_right_logical(lax.broadcasted_iota(jnp.int32, shp, 0) + i * blk, log2_p)
    cg = lax.shift_right_logical(lax.broadcasted_iota(jnp.int32, shp, 1), log2_cg)
    same = rg == cg
    wcr_ref[...] = jnp.where(same, ctr_ref[...], 0.0).astype(BF16)
    wci_ref[...] = jnp.where(same, -cti_ref[...], 0.0).astype(BF16)

    rows = lax.broadcasted_iota(jnp.int32, (SUBLANES, blk), 0)
    base_r, base_i = lb_r, lb_i
    for level in range(SCAN_LEVELS):
        pr, pi = base_r, base_i
        lpr = jnp.zeros((SUBLANES, blk), F32)
        lpi = jnp.zeros((SUBLANES, blk), F32)
        slot = 0
        for n in range(1, SUBLANES + 1):
            lpr = jnp.where(rows == n - 1, pr, lpr)
            lpi = jnp.where(rows == n - 1, pi, lpi)
            if n in SCAN_SHIFTS:
                amr_ref[level, slot] = jnp.where(rows >= n, pr, 0.0)
                ami_ref[level, slot] = jnp.where(rows >= n, pi, 0.0)
                slot += 1
            if n < SUBLANES:
                pr, pi = pr * base_r - pi * base_i, pr * base_i + pi * base_r
        lpr_ref[level] = lpr
        lpi_ref[level] = lpi
        base_r, base_i = pr, pi


def _ssm_prep(lam_re, lam_im, log_dt, b_re, b_im, c_re, c_im, nblk=4):
    g, p = lam_re.shape
    cgs = b_re.shape[-1]
    gp, gc = g * p, g * cgs
    blk = gp // nblk
    flat = lambda a: a.reshape(1, gp)
    ldt = flat(jnp.broadcast_to(log_dt[:, None], (g, p)))
    bt = lambda a: jnp.tile(a.transpose(0, 2, 1).reshape(gc, p), (1, g))
    ct = lambda a: jnp.tile(a.transpose(0, 2, 1).reshape(gp, cgs), (1, g))
    lane_blk = lambda shape: pl.BlockSpec(shape, lambda i: (0,) * (len(shape) - 1) + (i,))
    return pl.pallas_call(
        functools.partial(_ssm_prep_kernel, log2_cg=int(math.log2(cgs)), log2_p=int(math.log2(p))),
        grid=(nblk,),
        in_specs=[lane_blk((1, blk))] * 3 + [lane_blk((gc, blk))] * 2
                 + [pl.BlockSpec((blk, gc), lambda i: (i, 0))] * 2,
        out_specs=[lane_blk((gc, blk))] * 2 + [pl.BlockSpec((blk, gc), lambda i: (i, 0))] * 2
                  + [lane_blk((SCAN_LEVELS, SUBLANES, blk))] * 2
                  + [lane_blk((SCAN_LEVELS, len(SCAN_SHIFTS), SUBLANES, blk))] * 2
                  + [pl.BlockSpec((1, MXU_SCAN * (gc // nblk), blk), lambda i: (i, 0, 0))] * 2,
        out_shape=[jax.ShapeDtypeStruct((gc, gp), BF16)] * 2 + [jax.ShapeDtypeStruct((gp, gc), BF16)] * 2
                  + [jax.ShapeDtypeStruct((SCAN_LEVELS, SUBLANES, gp), F32)] * 2
                  + [jax.ShapeDtypeStruct((SCAN_LEVELS, len(SCAN_SHIFTS), SUBLANES, gp), F32)] * 2
                  + [jax.ShapeDtypeStruct((nblk, MXU_SCAN * (gc // nblk), blk), BF16)] * 2,
        compiler_params=_cparams(1),
        name="ssm_prep",
    )(flat(lam_re), flat(lam_im), ldt, bt(b_re), bt(b_im), ct(c_re), ct(c_im))


def _ssm_tail(hr, hi, u, wcr_ref, wci_ref, dsk_ref, wglu_ref):
    y = _dot(hr.astype(BF16), wcr_ref[...]) + _dot(hi.astype(BF16), wci_ref[...]) + dsk_ref[...] * u
    zs = _gelu(y)
    return zs * _sigmoid(_dot(zs.astype(BF16), wglu_ref[...]))


def _ssm_scan_kernel(u_ref, wsr_ref, wsi_ref, wcr_ref, wci_ref, lpr_ref, lpi_ref, amr_ref, ami_ref,
                     dsk_ref, wglu_ref, o_ref, hr_ref, hi_ref, xr_ref, xi_ref, cr_ref, ci_ref,
                     sr_ref, si_ref):
    tc = pl.program_id(1)
    lt = u_ref.shape[0]
    gp = lpr_ref.shape[-1]
    ng = lt // SUBLANES
    nblk, _, blk = wsr_ref.shape
    ch = u_ref.shape[1] // nblk

    @pl.when(tc == 0)
    def _init():
        cr_ref[...] = jnp.zeros_like(cr_ref)
        ci_ref[...] = jnp.zeros_like(ci_ref)

    u = u_ref[...]
    row = lax.broadcasted_iota(jnp.int32, (ng, SUBLANES, ch), 1)
    parts_r, parts_i = [], []
    for c in range(nblk):
        uc = u[:, c * ch:(c + 1) * ch].reshape(ng, SUBLANES, ch)
        lagged = [uc] + [jnp.where(row >= k, pltpu.roll(uc, k, axis=1), 0.0) for k in range(1, MXU_SCAN)]
        lhs = jnp.concatenate(lagged, axis=2).reshape(lt, MXU_SCAN * ch).astype(BF16)
        parts_r.append(_dot(lhs, wsr_ref[c]))
        parts_i.append(_dot(lhs, wsi_ref[c]))
    xr = jnp.concatenate(parts_r, axis=1).reshape(ng, SUBLANES, gp)
    xi = jnp.concatenate(parts_i, axis=1).reshape(ng, SUBLANES, gp)
    def scan8(ar3, ai3, level, first_shift):
        for idx, d in enumerate(SCAN_SHIFTS):
            if d < first_shift:
                continue
            mr = amr_ref[level, idx]
            mi = ami_ref[level, idx]
            sr = pltpu.roll(ar3, d, axis=1)
            si = pltpu.roll(ai3, d, axis=1)
            ar3, ai3 = ar3 + mr * sr - mi * si, ai3 + mr * si + mi * sr
        return ar3, ai3

    xr, xi = scan8(xr, xi, 0, MXU_SCAN)
    nl = gp // LANES
    tile = lambda a, l: a[:, l * LANES:(l + 1) * LANES]
    whole = lambda ref, rows: jnp.concatenate([ref[l, rows, :] for l in range(nl)], axis=1)
    xr, xi = xr.reshape(lt, gp), xi.reshape(lt, gp)
    for l in range(nl):
        xr_ref[l] = tile(xr, l)
        xi_ref[l] = tile(xi, l)

    last = SUBLANES - 1
    nsg = ng // SUBLANES
    ends = pl.ds(last, ng, stride=SUBLANES)
    er, ei = scan8(whole(xr_ref, ends).reshape(nsg, SUBLANES, gp),
                   whole(xi_ref, ends).reshape(nsg, SUBLANES, gp), 1, 1)
    l1r, l1i = lpr_ref[1], lpi_ref[1]
    hr, hi = cr_ref[...], ci_ref[...]
    sr_ref[last:SUBLANES, :] = hr[0:1, :]
    si_ref[last:SUBLANES, :] = hi[0:1, :]
    for j in range(nsg):
        gr = er[j] + l1r * hr - l1i * hi
        gi = ei[j] + l1r * hi + l1i * hr
        sr_ref[SUBLANES * (j + 1):SUBLANES * (j + 2), :] = gr
        si_ref[SUBLANES * (j + 1):SUBLANES * (j + 2), :] = gi
        hr = jnp.broadcast_to(gr[last:, :], gr.shape)
        hi = jnp.broadcast_to(gi[last:, :], gi.shape)
    cr_ref[...] = hr
    ci_ref[...] = hi
    hr_ref[0] = hr[0:1, :]
    hi_ref[0] = hi[0:1, :]

    l0r, l0i = lpr_ref[0], lpi_ref[0]
    for i in range(ng):
        rows = slice(i * SUBLANES, (i + 1) * SUBLANES)
        pr = jnp.broadcast_to(sr_ref[last + i:last + i + 1, :], (SUBLANES, gp))
        pi = jnp.broadcast_to(si_ref[last + i:last + i + 1, :], (SUBLANES, gp))
        dr = l0r * pr - l0i * pi
        di = l0r * pi + l0i * pr
        for l in range(nl):
            xr_ref[l, rows, :] += tile(dr, l)
            xi_ref[l, rows, :] += tile(di, l)
    everything = slice(None)
    o_ref[...] = _ssm_tail(whole(xr_ref, everything), whole(xi_ref, everything), u,
                           wcr_ref, wci_ref, dsk_ref, wglu_ref)


def _ssm_prompt(u, prep, dsk, wglu_bf, b, t, lt=256):
    _, _, wcr, wci, lpr, lpi, amr, ami, wsr, wsi = prep
    m, sw = u.shape
    gp = lpr.shape[-1]
    nt = t // lt
    row = pl.BlockSpec((lt, sw), lambda bi, i: (bi * nt + i, 0))
    hspec = pl.BlockSpec((1, 1, gp), lambda bi, i: (bi, 0, 0))
    return pl.pallas_call(
        _ssm_scan_kernel,
        grid=(b, nt),
        in_specs=[row, _const_spec(wsr.shape), _const_spec(wsi.shape), _const_spec((gp, sw)),
                  _const_spec((gp, sw)), _const_spec(lpr.shape), _const_spec(lpi.shape),
                  _const_spec(amr.shape), _const_spec(ami.shape),
                  _const_spec((1, sw)), _const_spec((sw, sw))],
        out_specs=[row, hspec, hspec],
        out_shape=[jax.ShapeDtypeStruct((m, sw), F32), jax.ShapeDtypeStruct((b, 1, gp), F32),
                   jax.ShapeDtypeStruct((b, 1, gp), F32)],
        scratch_shapes=[pltpu.VMEM((gp // LANES, lt, LANES), F32)] * 2 + [pltpu.VMEM((SUBLANES, gp), F32)] * 2
                       + [pltpu.VMEM((SUBLANES + lt // SUBLANES, gp), F32)] * 2,
        compiler_params=_cparams(2),
        name="ssm_scan",
    )(u, wsr, wsi, wcr, wci, lpr, lpi, amr, ami, dsk, wglu_bf)


def _ssm_step_kernel(u_ref, h0r_ref, h0i_ref, wbr_ref, wbi_ref, wcr_ref, wci_ref, lpr_ref, lpi_ref,
                     dsk_ref, wglu_ref, o_ref, hr_ref, hi_ref):
    u = u_ref[...]
    ub = u.astype(BF16)
    lr = lpr_ref[0, 0:1, :]
    li = lpi_ref[0, 0:1, :]
    h0r = h0r_ref[...]
    h0i = h0i_ref[...]
    hr = _dot(ub, wbr_ref[...]) + (lr * h0r - li * h0i)
    hi = _dot(ub, wbi_ref[...]) + (lr * h0i + li * h0r)
    hr_ref[...] = hr
    hi_ref[...] = hi
    o_ref[...] = _ssm_tail(hr, hi, u, wcr_ref, wci_ref, dsk_ref, wglu_ref)


def _ssm_step(u, h0r, h0i, prep, dsk, wglu_bf):
    wbr, wbi, wcr, wci, lpr, lpi = prep[:6]
    m, sw = u.shape
    gp = wbr.shape[1]
    return pl.pallas_call(
        _ssm_step_kernel,
        out_shape=[jax.ShapeDtypeStruct((m, sw), F32), jax.ShapeDtypeStruct((m, gp), F32),
                   jax.ShapeDtypeStruct((m, gp), F32)],
        compiler_params=pltpu.CompilerParams(vmem_limit_bytes=VMEM_LIMIT),
        name="ssm_step",
    )(u, h0r, h0i, wbr, wbi, wcr, wci, lpr, lpi, dsk, wglu_bf)


def _mix_residual(x, att, ssm, ga_ref, gs_ref, wo_ref, gpm_ref):
    an = _rms(att, ga_ref[...]).astype(BF16)
    sn = _rms(ssm, gs_ref[...]).astype(BF16)
    aw = an.shape[-1]
    y = _dot(an, wo_ref[0:aw, :]) + _dot(sn, wo_ref[aw:, :])
    return x + _rms(y, gpm_ref[...])


def _gated_mlp(xn, wg_ref, wu_ref, wd_ref, conv, fc):
    f = wg_ref.shape[1]
    acc = None
    pending = None
    for cs in [slice(c * fc, (c + 1) * fc) for c in range(f // fc)]:
        gate = _dot(xn, wg_ref[:, cs])
        up = _dot(xn, wu_ref[:, cs])
        if pending is not None:
            part = _dot(pending[0], wd_ref[pending[1], :])
            acc = part if acc is None else acc + part
        pending = ((_gelu(conv(gate, cs)) * up).astype(BF16), cs)
    part = _dot(pending[0], wd_ref[pending[1], :])
    return part if acc is None else acc + part


def _tail_seq_kernel(x_ref, att_ref, ssm_ref, ga_ref, gs_ref, wo_ref, gpm_ref,
                     g_ref, wg_ref, wu_ref, cw_ref, cb_ref, wd_ref, gpost_ref,
                     o_ref, cn_ref, gate_ref, *, fc, blocks_per_seq):
    i = pl.program_id(0)
    bm = x_ref.shape[0]
    hist = SUBLANES
    first = (i % blocks_per_seq) == 0

    @pl.when(first)
    def _zero_hist():
        gate_ref[0:hist, :] = jnp.zeros((hist, gate_ref.shape[1]), F32)

    @pl.when(jnp.logical_not(first))
    def _keep_hist():
        gate_ref[0:hist, :] = gate_ref[bm:bm + hist, :]

    x1 = _mix_residual(x_ref[...], att_ref[...], ssm_ref[...], ga_ref, gs_ref, wo_ref, gpm_ref)
    xn = _rms(x1, g_ref[...]).astype(BF16)

    def conv(gate, cs):
        gate_ref[hist:hist + bm, cs] = gate
        g1 = gate_ref[hist - 1:hist - 1 + bm, cs]
        g2 = gate_ref[hist - 2:hist - 2 + bm, cs]
        return cb_ref[:, cs] + g2 * cw_ref[0:1, cs] + g1 * cw_ref[1:2, cs] + gate * cw_ref[2:3, cs]

    acc = _gated_mlp(xn, wg_ref, wu_ref, wd_ref, conv, fc)
    cn_ref[0] = gate_ref[hist + bm - 2:hist + bm, :]
    o_ref[...] = x1 + _rms(acc, gpost_ref[...])


def _resident(shape):
    return pl.BlockSpec(shape, lambda *_: (0,) * len(shape), pipeline_mode=pl.Buffered(1))


def _tail_weight_specs(d, aw, sw, f):
    return [_const_spec((1, aw)), _const_spec((1, sw)), _resident((aw + sw, d)), _const_spec((1, d)),
            _const_spec((1, d)), _resident((d, f)), _resident((d, f)), _const_spec((3, f)),
            _const_spec((1, f)), _resident((f, d)), _const_spec((1, d))]


def _tail_seq(x, att, ssm, weights, b, t, bm=256, fc=256):
    m, d = x.shape
    aw, sw = att.shape[1], ssm.shape[1]
    f = weights[5].shape[1]
    bps = t // bm
    row = lambda w: pl.BlockSpec((bm, w), lambda i: (i, 0))
    return pl.pallas_call(
        functools.partial(_tail_seq_kernel, fc=fc, blocks_per_seq=bps),
        grid=(m // bm,),
        in_specs=[row(d), row(aw), row(sw)] + _tail_weight_specs(d, aw, sw, f),
        out_specs=[row(d), pl.BlockSpec((1, 2, f), lambda i: (i // bps, 0, 0))],
        out_shape=[jax.ShapeDtypeStruct((m, d), F32), jax.ShapeDtypeStruct((b, 2, f), F32)],
        scratch_shapes=[pltpu.VMEM((SUBLANES + bm, f), F32)],
        compiler_params=_cparams(1),
        name="tail_seq",
    )(x, att, ssm, *weights)


def _tail_step_kernel(x_ref, att_ref, ssm_ref, ga_ref, gs_ref, wo_ref, gpm_ref,
                      g_ref, wg_ref, wu_ref, cw_ref, cb_ref, wd_ref, gpost_ref, p0_ref, p1_ref,
                      o_ref, gate_ref, *, fc):
    x1 = _mix_residual(x_ref[...], att_ref[...], ssm_ref[...], ga_ref, gs_ref, wo_ref, gpm_ref)
    xn = _rms(x1, g_ref[...]).astype(BF16)

    def conv(gate, cs):
        gate_ref[:, cs] = gate
        return (cb_ref[:, cs] + p0_ref[:, cs] * cw_ref[0:1, cs] + p1_ref[:, cs] * cw_ref[1:2, cs]
                + gate * cw_ref[2:3, cs])

    acc = _gated_mlp(xn, wg_ref, wu_ref, wd_ref, conv, fc)
    o_ref[...] = x1 + _rms(acc, gpost_ref[...])


def _tail_step(x, att, ssm, weights, p0, p1, fc=256):
    m, d = x.shape
    f = weights[5].shape[1]
    return pl.pallas_call(
        functools.partial(_tail_step_kernel, fc=fc),
        out_shape=[jax.ShapeDtypeStruct((m, d), F32), jax.ShapeDtypeStruct((m, f), F32)],
        compiler_params=pltpu.CompilerParams(vmem_limit_bytes=VMEM_LIMIT),
        name="tail_step",
    )(x, att, ssm, *weights, p0, p1)


def kernel(x_prompt, x_sample, cache_k, cache_v, state_ssm_re, state_ssm_im, state_ffn_conv, page_table,
           g_pre_mix, w_in, sb_bias, g_att_out, lam_re, lam_im, log_dt, b_re, b_im, c_re, c_im, d_skip,
           w_glu, g_ssm_out, w_out, g_post_mix, g_pre_ffn, w_gate, w_up, conv_w, conv_b, w_down,
           g_post_ffn):
    depth = w_in.shape[0]
    assert depth == 1, "single-layer step"
    b, t, d = x_prompt.shape
    bs, ts, _ = x_sample.shape
    assert ts == 1, "sample group advances one token per sequence"
    n_heads, hd = cache_k.shape[-2:]
    assert hd == HEAD_DIM
    aw = n_heads * hd
    g, p = lam_re.shape[1:]
    l = 0

    row = lambda a: a[l].reshape(1, -1)
    w_in_bf = w_in[l].astype(BF16)
    wglu_bf = w_glu[l].astype(BF16)
    wo_bf = w_out[l].astype(BF16)
    wg_bf = w_gate[l].astype(BF16)
    wu_bf = w_up[l].astype(BF16)
    wd_bf = w_down[l].astype(BF16)
    bias = sb_bias[l]
    dsk = row(d_skip)
    tail_w = (row(g_att_out), row(g_ssm_out), wo_bf, row(g_post_mix), row(g_pre_ffn), wg_bf, wu_bf,
              conv_w[l], row(conv_b), wd_bf, row(g_post_ffn))
    prep = _ssm_prep(lam_re[l], lam_im[l], log_dt[l], b_re[l], b_im[l], c_re[l], c_im[l])

    xp = x_prompt.reshape(b * t, d)
    xs = x_sample.reshape(bs, d)
    qp, ktp, vtp, up = _inproj_seq(xp, row(g_pre_mix), w_in_bf, aw, b, t, bm=512)
    qs, ks, vs, us = _inproj(xs, row(g_pre_mix), w_in_bf, aw, bm=bs)
    att_p, att_s = _attention(qp, ktp, vtp, qs, cache_k[l].transpose(0, 2, 3, 1),
                              cache_v[l].transpose(0, 2, 3, 1), page_table, bias, b, t)
    ssm_p, hrp, hip = _ssm_prompt(up, prep, dsk, wglu_bf, b, t)
    yp, conv_p = _tail_seq(xp, att_p, ssm_p, tail_w, b, t)

    ssm_s, hrs, his = _ssm_step(us, state_ssm_re[l].reshape(bs, g * p), state_ssm_im[l].reshape(bs, g * p),
                                prep, dsk, wglu_bf)
    prev = state_ffn_conv[l]
    ys, gate_s = _tail_step(xs, att_s, ssm_s, tail_w, prev[:, 0], prev[:, 1])
    conv_s = jnp.stack([prev[:, 1], gate_s], axis=1)

    heads = lambda a, n, s: a.reshape(1, n, s, n_heads, hd)
    heads_t = lambda a: a.reshape(1, b, n_heads, hd, t).transpose(0, 1, 4, 2, 3)
    state = lambda a, n: a.reshape(1, n, g, p)
    return (yp.reshape(b, t, d), ys.reshape(bs, 1, d),
            heads_t(ktp), heads_t(vtp), heads(ks, bs, 1), heads(vs, bs, 1),
            state(hrp, b), state(hip, b), state(hrs, bs), state(his, bs),
            conv_p[None], conv_s[None])
```

```python
import functools
import math

import jax
import jax.numpy as jnp
from jax import lax
from jax.experimental import pallas as pl
from jax.experimental.pallas import tpu as pltpu

F32 = jnp.float32
BF16 = jnp.bfloat16

RMS_EPS = 1e-6
HEAD_DIM = 64
LANES = 128
SUBLANES = 8
HEADS_PER_BLOCK = LANES // HEAD_DIM
SQRT_2_OVER_PI = math.sqrt(2.0 / math.pi)
LOG2E = math.log2(math.e)
SUFFIX_LAG = 3
WEIGH_LAG = 7
MXU_SCAN = 4
SCAN_SHIFTS = (1, 2, 4)
SCAN_LEVELS = 2
DECODE_GROUP = 16
DECODE_SPREAD = 16
TAIL_SUB_BLOCKS = 2
V7X_VMEM_BYTES = 64 * 1024 * 1024
VMEM_LIMIT = V7X_VMEM_BYTES - 8 * 1024 * 1024


def _cparams(n_axes):
    return pltpu.CompilerParams(dimension_semantics=("arbitrary",) * n_axes,
                                vmem_limit_bytes=VMEM_LIMIT)


def _rms(x, g):
    return x * lax.rsqrt(jnp.mean(x * x, axis=-1, keepdims=True) + RMS_EPS) * g


def _gelu(x):
    return 0.5 * x * (1.0 + jnp.tanh(SQRT_2_OVER_PI * (x + 0.044715 * (x * x * x))))


def _sigmoid(x):
    return 1.0 / (1.0 + jnp.exp(-x))


def _softplus2(zz):
    return jnp.maximum(zz, 0.0) + jnp.log(1.0 + jnp.exp2(-jnp.abs(zz))) * LOG2E


def _split_bf16(x):
    hi = x.astype(BF16)
    return hi, (x - hi.astype(F32)).astype(BF16)


def _dot(a, b):
    return jnp.dot(a, b, preferred_element_type=F32)


def _dot_nt(a, b):
    return lax.dot_general(a, b, (((1,), (1,)), ((), ())), preferred_element_type=F32)


def _const_spec(shape):
    return pl.BlockSpec(shape, lambda *_: (0,) * len(shape))


def _inproj_kernel(x_ref, g_ref, w_ref, q_ref, k_ref, v_ref, u_ref):
    xn = _rms(x_ref[...], g_ref[...]).astype(BF16)
    aw = q_ref.shape[-1]
    q_ref[...] = _dot(xn, w_ref[:, 0:aw])
    k_ref[...] = _dot(xn, w_ref[:, aw:2 * aw])
    v_ref[...] = _dot(xn, w_ref[:, 2 * aw:3 * aw])
    u_ref[...] = _dot(xn, w_ref[:, 3 * aw:])


def _inproj(x, g, w_bf, att_w, bm):
    m, d = x.shape
    n = w_bf.shape[1]
    sw = n - 3 * att_w
    row = lambda i: (i, 0)
    return pl.pallas_call(
        _inproj_kernel,
        grid=(m // bm,),
        in_specs=[pl.BlockSpec((bm, d), row), _const_spec((1, d)), _const_spec((d, n))],
        out_specs=[pl.BlockSpec((bm, att_w), row)] * 3 + [pl.BlockSpec((bm, sw), row)],
        out_shape=[jax.ShapeDtypeStruct((m, att_w), F32)] * 3 + [jax.ShapeDtypeStruct((m, sw), F32)],
        compiler_params=_cparams(1),
        name="inproj",
    )(x, g, w_bf)


def _inproj_seq_kernel(x_ref, g_ref, wq_ref, wkt_ref, wvt_ref, wu_ref, q_ref, kt_ref, vt_ref, u_ref):
    xn = _rms(x_ref[...], g_ref[...]).astype(BF16)
    q_ref[...] = _dot(xn, wq_ref[...])
    kt_ref[...] = _dot_nt(wkt_ref[...], xn)
    vt_ref[...] = _dot_nt(wvt_ref[...], xn)
    u_ref[...] = _dot(xn, wu_ref[...])


def _inproj_seq(x, g, w_bf, att_w, b, t, bm):
    m, d = x.shape
    n = w_bf.shape[1]
    sw = n - 3 * att_w
    nb = t // bm
    row = lambda i: (i, 0)
    tspec = pl.BlockSpec((att_w, bm), lambda i: (i // nb, i % nb))
    wq, wk, wv, wu = (w_bf[:, 0:att_w], w_bf[:, att_w:2 * att_w], w_bf[:, 2 * att_w:3 * att_w],
                      w_bf[:, 3 * att_w:])
    return pl.pallas_call(
        _inproj_seq_kernel,
        grid=(m // bm,),
        in_specs=[pl.BlockSpec((bm, d), row), _const_spec((1, d)), _const_spec((d, att_w)),
                  _const_spec((att_w, d)), _const_spec((att_w, d)), _const_spec((d, sw))],
        out_specs=[pl.BlockSpec((bm, att_w), row), tspec, tspec, pl.BlockSpec((bm, sw), row)],
        out_shape=[jax.ShapeDtypeStruct((m, att_w), F32), jax.ShapeDtypeStruct((b * att_w, t), F32),
                   jax.ShapeDtypeStruct((b * att_w, t), F32), jax.ShapeDtypeStruct((m, sw), F32)],
        compiler_params=_cparams(1),
        name="inproj_seq",
    )(x, g, wq, wk.T, wv.T, wu)


def _decode_stages(step, pt_ref, qs_ref, biasb_ref, ll_ref, ck_ref, cv_ref, os_ref,
                   kpg_ref, vpg_ref, sem, dacc_ref, dcarry_ref, *, n_steps):
    bs, npg = pt_ref.shape
    _, pps, n_heads, hd, page = kpg_ref.shape
    aw = n_heads * hd
    sps = n_steps // bs
    grp = step % sps
    slot = step % 2

    def page_copies(s, slot_):
        seq = s // sps
        first_page = npg - 1 - (s % sps) * pps
        for i in range(pps):
            pid = pt_ref[seq, first_page - i]
            yield pltpu.make_async_copy(ck_ref.at[pid], kpg_ref.at[slot_, i], sem.at[slot_, 0])
            yield pltpu.make_async_copy(cv_ref.at[pid], vpg_ref.at[slot_, i], sem.at[slot_, 1])

    @pl.when(step == 0)
    def _fetch_first():
        for cp in page_copies(step, slot):
            cp.start()

    @pl.when(step + 1 < n_steps)
    def _fetch_next():
        for cp in page_copies(step + 1, 1 - slot):
            cp.start()

    for cp in page_copies(step, slot):
        cp.wait()

    @pl.when(grp == 0)
    def _new_sequence():
        dacc_ref[...] = jnp.zeros_like(dacc_ref)
        dcarry_ref[...] = jnp.zeros_like(dcarry_ref)

    head_of_lane = lax.shift_right_logical(lax.broadcasted_iota(jnp.int32, (n_heads, aw), 1),
                                           HEAD_DIM.bit_length() - 1)
    own = head_of_lane == lax.broadcasted_iota(jnp.int32, (n_heads, aw), 0)
    qrows = jnp.where(own, qs_ref[0] * (HEAD_DIM ** -0.5 * LOG2E), 0.0).astype(BF16)
    bias2 = biasb_ref[...] * LOG2E
    paged = lambda ref, i: ref[slot, i].reshape(aw, page).astype(BF16)
    rows = lambda a, i: a[i * n_heads:(i + 1) * n_heads]

    gsz = min(DECODE_GROUP, pps)
    n_groups = pps // gsz
    per_step = -(-pps // DECODE_SPREAD)
    zz = [None] * pps
    sums = [None] * n_groups
    weights = [None] * n_groups
    state = {"acc": None, "carry": None}

    def logits(i):
        zz[i] = _dot(qrows, paged(kpg_ref, i)) + bias2

    def suffix_sums(g):
        z = jnp.concatenate(zz[g * gsz:(g + 1) * gsz], axis=0)
        p = _softplus2(z)
        hi, lo = _split_bf16(p)
        cs = _dot(jnp.concatenate([hi, lo], axis=1), ll_ref[...])
        sums[g] = (z - p, cs[:, :page], cs[:, page:])

    def weigh(g):
        if state["carry"] is None:
            state["carry"] = dcarry_ref[...]
            state["acc"] = dacc_ref[...]
        logsig, later, tot = sums[g]
        carries = []
        for i in range(gsz):
            carries.append(state["carry"])
            state["carry"] = state["carry"] + rows(tot, i)
        weights[g] = jnp.exp2(logsig - later - jnp.concatenate(carries, axis=0))

    def values(i):
        g, j = divmod(i, gsz)
        state["acc"] = state["acc"] + _dot_nt(rows(weights[g], j).astype(BF16), paged(vpg_ref, i))

    def finish():
        dacc_ref[...] = state["acc"]
        dcarry_ref[...] = state["carry"]
        os_ref[0] = jnp.sum(jnp.where(own, state["acc"], 0.0), axis=0, keepdims=True)

    plan = {}
    at = lambda s, f, *a: plan.setdefault(s, []).append(functools.partial(f, *a))
    ready = 0
    for g in range(n_groups):
        for i in range(g * gsz, (g + 1) * gsz):
            at(i // per_step, logits, i)
        s_sum = max(ready, ((g + 1) * gsz - 1) // per_step + SUFFIX_LAG + 1)
        at(s_sum, suffix_sums, g)
        s_w = s_sum + SUFFIX_LAG + 1
        at(s_w, weigh, g)
        for j in range(gsz):
            at(s_w + j // per_step, values, g * gsz + j)
        ready = s_w + (gsz - 1) // per_step + 1
    at(ready, finish)
    return plan


def _attn_kernel(pt_ref, bias_ref, q_ref, ktin_ref, vtin_ref, uu_ref, qs_ref, biasb_ref, ll_ref,
                 ck_ref, cv_ref, o_ref, os_ref, kb_ref, vt_ref, acc_ref, kpg_ref, vpg_ref, sem,
                 dacc_ref, dcarry_ref, *, bq, bk, qw, t, n_steps):
    hp = pl.program_id(1)
    qi = pl.program_id(2)
    nk = t // bk
    r = bq // bk
    step = (pl.program_id(0) * pl.num_programs(1) + hp) * pl.num_programs(2) + qi
    decode_plan = _decode_stages(step, pt_ref, qs_ref, biasb_ref, ll_ref, ck_ref, cv_ref, os_ref,
                                 kpg_ref, vpg_ref, sem, dacc_ref, dcarry_ref, n_steps=n_steps)
    bias_lane = [(1 - hh) * HEAD_DIM for hh in range(HEADS_PER_BLOCK)]

    def own_lanes(lane, hh):
        return (lane >= hh * HEAD_DIM) & (lane < (hh + 1) * HEAD_DIM)

    @pl.when(qi == 0)
    def _prep():
        lane = lax.broadcasted_iota(jnp.int32, (bk, LANES), 1)
        for j in range(nk):
            cols = slice(j * bk, (j + 1) * bk)
            kblk = ktin_ref[:, cols].T
            for hh in range(HEADS_PER_BLOCK):
                ones = (lane == bias_lane[hh]) | (lane == bias_lane[hh] + 1)
                kb_ref[hh, j] = jnp.where(own_lanes(lane, hh), kblk,
                                          jnp.where(ones, 1.0, 0.0)).astype(BF16)
            vt_ref[j] = vtin_ref[:, cols].astype(BF16)

    lane_q = lax.broadcasted_iota(jnp.int32, (bq, LANES), 1)
    q = q_ref[...] * (HEAD_DIM ** -0.5 * LOG2E)
    qa = []
    for hh in range(HEADS_PER_BLOCK):
        b2 = jnp.full((bq, LANES), bias_ref[HEADS_PER_BLOCK * hp + hh] * LOG2E, F32)
        b_hi = b2.astype(BF16).astype(F32)
        aug = jnp.where(lane_q == bias_lane[hh], b_hi,
                        jnp.where(lane_q == bias_lane[hh] + 1, b2 - b_hi, 0.0))
        qa.append(jnp.where(own_lanes(lane_q, hh), q, aug).astype(BF16))
    uu = uu_ref[...]

    nqs = bq // qw

    def tiles(first_block, carries, diagonal, riders):
        order = []
        for d in reversed(range(r)):
            for hh in range(HEADS_PER_BLOCK):
                for qs in range(nqs):
                    if not diagonal or (qs + 1) * qw - 1 > d * bk:
                        order.append((hh, d, qs))
        n = len(order)
        carries = list(carries)
        zz = [None] * n
        later = [None] * n
        ptop = [None] * n

        def visible(d, qs):
            if not diagonal or qs * qw >= (d + 1) * bk:
                return None
            return (lax.broadcasted_iota(jnp.int32, (bk, qw), 0) + d * bk
                    < lax.broadcasted_iota(jnp.int32, (bk, qw), 1) + qs * qw)

        def logits(i):
            hh, d, qs = order[i]
            zz[i] = _dot_nt(kb_ref[hh, first_block + d], qa[hh][qs * qw:(qs + 1) * qw, :])

        def suffix_sums(i):
            hh, d, qs = order[i]
            p = _softplus2(zz[i])
            vis = visible(d, qs)
            if vis is not None:
                p = jnp.where(vis, p, 0.0)
            later[i] = _dot(uu, p.astype(BF16))
            ptop[i] = p[0:1, :]
            zz[i] = zz[i] - p

        def weigh(i):
            hh, d, qs = order[i]
            c = hh * nqs + qs
            w = jnp.exp2(zz[i] - later[i] - carries[c])
            vis = visible(d, qs)
            if vis is not None:
                w = jnp.where(vis, w, 0.0)
            carries[c] = carries[c] + (later[i][0:1, :] + ptop[i])
            acc_ref[hh, :, qs * qw:(qs + 1) * qw] += _dot(
                vt_ref[first_block + d, hh * HEAD_DIM:(hh + 1) * HEAD_DIM, :], w.astype(BF16))
            zz[i] = later[i] = ptop[i] = None

        for s in range(max(n + WEIGH_LAG, max(riders, default=-1) + 1)):
            if s < n:
                logits(s)
            if 0 <= s - SUFFIX_LAG < n:
                suffix_sums(s - SUFFIX_LAG)
            if 0 <= s - WEIGH_LAG < n:
                weigh(s - WEIGH_LAG)
            for extra in riders.get(s, ()):
                extra()
        return tuple(carries)

    acc_ref[...] = jnp.zeros_like(acc_ref)
    zero = jnp.zeros((1, qw), F32)
    carries = tiles(qi * r, (zero,) * (HEADS_PER_BLOCK * nqs), True, decode_plan)
    lax.fori_loop(0, qi, lambda it, c: tiles((qi - 1 - it) * r, c, False, {}), carries)
    o_ref[...] = acc_ref[...].reshape(HEADS_PER_BLOCK * HEAD_DIM, bq).T


def _suffix_ones(bk):
    return (jnp.arange(bk)[None, :] > jnp.arange(bk)[:, None]).astype(BF16)


def _later_ones(page):
    sp = jnp.arange(2 * page)[:, None] % page
    s = jnp.arange(page)[None, :]
    low = (sp > s).astype(BF16)
    return jnp.concatenate([low, jnp.ones((2 * page, page), BF16)], axis=1)


def _attention(q, kt, vt, qs, cache_kt, cache_vt, page_table, bias, b, t, bq=1024, bk=128, qw=256):
    m, aw = q.shape
    bs = qs.shape[0]
    n_phys, n_heads, hd, page = cache_kt.shape
    npg = page_table.shape[1]
    nq = t // bq
    nk = t // bk
    hp = aw // LANES
    n_steps = b * hp * nq
    sps = n_steps // bs
    assert sps * bs == n_steps and npg % sps == 0, "sample sequences must tile the prompt grid"
    pps = npg // sps
    seq_of = lambda bi, h, i: ((bi * hp + h) * nq + i) // sps
    qspec = pl.BlockSpec((bq, LANES), lambda bi, h, i, pt: (bi * nq + i, h))
    kvspec = pl.BlockSpec((LANES, t), lambda bi, h, i, pt: (bi * hp + h, 0))
    sspec = pl.BlockSpec((1, 1, aw), lambda bi, h, i, pt: (seq_of(bi, h, i), 0, 0))
    const = lambda shape: pl.BlockSpec(shape, lambda bi, h, i, pt: (0,) * len(shape))
    grid_spec = pltpu.PrefetchScalarGridSpec(
        num_scalar_prefetch=1,
        grid=(b, hp, nq),
        in_specs=[pl.BlockSpec(memory_space=pltpu.SMEM), qspec, kvspec, kvspec, const((bk, bk)),
                  sspec, const((n_heads, LANES)), const((2 * page, 2 * page)),
                  pl.BlockSpec(memory_space=pl.ANY), pl.BlockSpec(memory_space=pl.ANY)],
        out_specs=[qspec, sspec],
        scratch_shapes=[pltpu.VMEM((HEADS_PER_BLOCK, nk, bk, LANES), BF16),
                        pltpu.VMEM((nk, LANES, bk), BF16),
                        pltpu.VMEM((HEADS_PER_BLOCK, HEAD_DIM, bq), F32),
                        pltpu.VMEM((2, pps, n_heads, hd, page), F32),
                        pltpu.VMEM((2, pps, n_heads, hd, page), F32),
                        pltpu.SemaphoreType.DMA((2, 2)),
                        pltpu.VMEM((n_heads, aw), F32),
                        pltpu.VMEM((n_heads, page), F32)],
    )
    att, att_s = pl.pallas_call(
        functools.partial(_attn_kernel, bq=bq, bk=bk, qw=qw, t=t, n_steps=n_steps),
        grid_spec=grid_spec,
        out_shape=[jax.ShapeDtypeStruct((m, aw), F32), jax.ShapeDtypeStruct((bs, 1, aw), F32)],
        compiler_params=_cparams(3),
        name="attention",
    )(page_table, bias, q, kt, vt, _suffix_ones(bk), qs.reshape(bs, 1, aw),
      jnp.broadcast_to(bias[:, None], (n_heads, LANES)), _later_ones(page), cache_kt, cache_vt)
    return att, att_s.reshape(bs, aw)


def _ssm_prep_kernel(lr_ref, li_ref, ldt_ref, btr_ref, bti_ref, ctr_ref, cti_ref,
                     wbr_ref, wbi_ref, wcr_ref, wci_ref, lpr_ref, lpi_ref, amr_ref, ami_ref,
                     wsr_ref, wsi_ref, *, log2_cg, log2_p):
    i = pl.program_id(0)
    blk = lr_ref.shape[-1]
    lam_r = lr_ref[...]
    lam_i = li_ref[...]
    dt = jnp.exp(ldt_ref[...])
    mag = jnp.exp(lam_r * dt)
    ang = lam_i * dt
    lb_r = mag * jnp.cos(ang)
    lb_i = mag * jnp.sin(ang)
    nr = lb_r - 1.0
    den = lam_r * lam_r + lam_i * lam_i
    cr = (nr * lam_r + lb_i * lam_i) / den
    ci = (lb_i * lam_r - nr * lam_i) / den

    shp = btr_ref.shape
    rg = lax.shift_right_logical(lax.broadcasted_iota(jnp.int32, shp, 0), log2_cg)
    cg = lax.shift_right_logical(lax.broadcasted_iota(jnp.int32, shp, 1) + i * blk, log2_p)
    same = rg == cg
    btr = btr_ref[...]
    bti = bti_ref[...]
    wbr_ref[...] = jnp.where(same, cr * btr - ci * bti, 0.0).astype(BF16)
    wbi_ref[...] = jnp.where(same, cr * bti + ci * btr, 0.0).astype(BF16)

    ch = wsr_ref.shape[1] // MXU_SCAN
    own = pl.ds(pl.multiple_of(i * ch, ch), ch)
    same = (lax.shift_right_logical(lax.broadcasted_iota(jnp.int32, (ch, blk), 0), log2_cg)
            == lax.shift_right_logical(lax.broadcasted_iota(jnp.int32, (ch, blk), 1), log2_p))
    wr = jnp.where(same, cr * btr_ref[own, :] - ci * bti_ref[own, :], 0.0)
    wi = jnp.where(same, cr * bti_ref[own, :] + ci * btr_ref[own, :], 0.0)
    for k in range(MXU_SCAN):
        wsr_ref[0, k * ch:(k + 1) * ch, :] = wr.astype(BF16)
        wsi_ref[0, k * ch:(k + 1) * ch, :] = wi.astype(BF16)
        wr, wi = wr * lb_r - wi * lb_i, wr * lb_i + wi * lb_r

    shp = ctr_ref.shape
    rg = lax.shift_right_logical(lax.broadcasted_iota(jnp.int32, shp, 0) + i * blk, log2_p)
    cg = lax.shift_right_logical(lax.broadcasted_iota(jnp.int32, shp, 1), log2_cg)
    same = rg == cg
    wcr_ref[...] = jnp.where(same, ctr_ref[...], 0.0).astype(BF16)
    wci_ref[...] = jnp.where(same, -cti_ref[...], 0.0).astype(BF16)

    rows = lax.broadcasted_iota(jnp.int32, (SUBLANES, blk), 0)
    base_r, base_i = lb_r, lb_i
    for level in range(SCAN_LEVELS):
        pr, pi = base_r, base_i
        lpr = jnp.zeros((SUBLANES, blk), F32)
        lpi = jnp.zeros((SUBLANES, blk), F32)
        slot = 0
        for n in range(1, SUBLANES + 1):
            lpr = jnp.where(rows == n - 1, pr, lpr)
            lpi = jnp.where(rows == n - 1, pi, lpi)
            if n in SCAN_SHIFTS:
                amr_ref[level, slot] = jnp.where(rows >= n, pr, 0.0)
                ami_ref[level, slot] = jnp.where(rows >= n, pi, 0.0)
                slot += 1
            if n < SUBLANES:
                pr, pi = pr * base_r - pi * base_i, pr * base_i + pi * base_r
        lpr_ref[level] = lpr
        lpi_ref[level] = lpi
        base_r, base_i = pr, pi


def _ssm_prep(lam_re, lam_im, log_dt, b_re, b_im, c_re, c_im, nblk=4):
    g, p = lam_re.shape
    cgs = b_re.shape[-1]
    gp, gc = g * p, g * cgs
    blk = gp // nblk
    flat = lambda a: a.reshape(1, gp)
    ldt = flat(jnp.broadcast_to(log_dt[:, None], (g, p)))
    bt = lambda a: jnp.tile(a.transpose(0, 2, 1).reshape(gc, p), (1, g))
    ct = lambda a: jnp.tile(a.transpose(0, 2, 1).reshape(gp, cgs), (1, g))
    lane_blk = lambda shape: pl.BlockSpec(shape, lambda i: (0,) * (len(shape) - 1) + (i,))
    return pl.pallas_call(
        functools.partial(_ssm_prep_kernel, log2_cg=int(math.log2(cgs)), log2_p=int(math.log2(p))),
        grid=(nblk,),
        in_specs=[lane_blk((1, blk))] * 3 + [lane_blk((gc, blk))] * 2
                 + [pl.BlockSpec((blk, gc), lambda i: (i, 0))] * 2,
        out_specs=[lane_blk((gc, blk))] * 2 + [pl.BlockSpec((blk, gc), lambda i: (i, 0))] * 2
                  + [lane_blk((SCAN_LEVELS, SUBLANES, blk))] * 2
                  + [lane_blk((SCAN_LEVELS, len(SCAN_SHIFTS), SUBLANES, blk))] * 2
                  + [pl.BlockSpec((1, MXU_SCAN * (gc // nblk), blk), lambda i: (i, 0, 0))] * 2,
        out_shape=[jax.ShapeDtypeStruct((gc, gp), BF16)] * 2 + [jax.ShapeDtypeStruct((gp, gc), BF16)] * 2
                  + [jax.ShapeDtypeStruct((SCAN_LEVELS, SUBLANES, gp), F32)] * 2
                  + [jax.ShapeDtypeStruct((SCAN_LEVELS, len(SCAN_SHIFTS), SUBLANES, gp), F32)] * 2
                  + [jax.ShapeDtypeStruct((nblk, MXU_SCAN * (gc // nblk), blk), BF16)] * 2,
        compiler_params=_cparams(1),
        name="ssm_prep",
    )(flat(lam_re), flat(lam_im), ldt, bt(b_re), bt(b_im), ct(c_re), ct(c_im))


def _ssm_tail(hr, hi, u, wcr_ref, wci_ref, dsk_ref, wglu_ref):
    y = _dot(hr.astype(BF16), wcr_ref[...]) + _dot(hi.astype(BF16), wci_ref[...]) + dsk_ref[...] * u
    zs = _gelu(y)
    return zs * _sigmoid(_dot(zs.astype(BF16), wglu_ref[...]))


def _ssm_scan_kernel(u_ref, wsr_ref, wsi_ref, wcr_ref, wci_ref, lpr_ref, lpi_ref, amr_ref, ami_ref,
                     dsk_ref, wglu_ref, o_ref, hr_ref, hi_ref, xr_ref, xi_ref, cr_ref, ci_ref,
                     sr_ref, si_ref):
    tc = pl.program_id(1)
    lt = u_ref.shape[0]
    gp = lpr_ref.shape[-1]
    ng = lt // SUBLANES
    nblk, _, blk = wsr_ref.shape
    ch = u_ref.shape[1] // nblk

    @pl.when(tc == 0)
    def _init():
        cr_ref[...] = jnp.zeros_like(cr_ref)
        ci_ref[...] = jnp.zeros_like(ci_ref)

    u = u_ref[...]
    row = lax.broadcasted_iota(jnp.int32, (ng, SUBLANES, ch), 1)
    parts_r, parts_i = [], []
    for c in range(nblk):
        uc = u[:, c * ch:(c + 1) * ch].reshape(ng, SUBLANES, ch)
        lagged = [uc] + [jnp.where(row >= k, pltpu.roll(uc, k, axis=1), 0.0) for k in range(1, MXU_SCAN)]
        lhs = jnp.concatenate(lagged, axis=2).reshape(lt, MXU_SCAN * ch).astype(BF16)
        parts_r.append(_dot(lhs, wsr_ref[c]))
        parts_i.append(_dot(lhs, wsi_ref[c]))
    xr = jnp.concatenate(parts_r, axis=1).reshape(ng, SUBLANES, gp)
    xi = jnp.concatenate(parts_i, axis=1).reshape(ng, SUBLANES, gp)
    def scan8(ar3, ai3, level, first_shift):
        for idx, d in enumerate(SCAN_SHIFTS):
            if d < first_shift:
                continue
            mr = amr_ref[level, idx]
            mi = ami_ref[level, idx]
            sr = pltpu.roll(ar3, d, axis=1)
            si = pltpu.roll(ai3, d, axis=1)
            ar3, ai3 = ar3 + mr * sr - mi * si, ai3 + mr * si + mi * sr
        return ar3, ai3

    xr, xi = scan8(xr, xi, 0, MXU_SCAN)
    nl = gp // LANES
    tile = lambda a, l: a[:, l * LANES:(l + 1) * LANES]
    whole = lambda ref, rows: jnp.concatenate([ref[l, rows, :] for l in range(nl)], axis=1)
    xr, xi = xr.reshape(lt, gp), xi.reshape(lt, gp)
    for l in range(nl):
        xr_ref[l] = tile(xr, l)
        xi_ref[l] = tile(xi, l)

    last = SUBLANES - 1
    nsg = ng // SUBLANES
    ends = pl.ds(last, ng, stride=SUBLANES)
    er, ei = scan8(whole(xr_ref, ends).reshape(nsg, SUBLANES, gp),
                   whole(xi_ref, ends).reshape(nsg, SUBLANES, gp), 1, 1)
    l1r, l1i = lpr_ref[1], lpi_ref[1]
    hr, hi = cr_ref[...], ci_ref[...]
    sr_ref[last:SUBLANES, :] = hr[0:1, :]
    si_ref[last:SUBLANES, :] = hi[0:1, :]
    for j in range(nsg):
        gr = er[j] + l1r * hr - l1i * hi
        gi = ei[j] + l1r * hi + l1i * hr
        sr_ref[SUBLANES * (j + 1):SUBLANES * (j + 2), :] = gr
        si_ref[SUBLANES * (j + 1):SUBLANES * (j + 2), :] = gi
        hr = jnp.broadcast_to(gr[last:, :], gr.shape)
        hi = jnp.broadcast_to(gi[last:, :], gi.shape)
    cr_ref[...] = hr
    ci_ref[...] = hi
    hr_ref[0] = hr[0:1, :]
    hi_ref[0] = hi[0:1, :]

    l0r, l0i = lpr_ref[0], lpi_ref[0]
    for i in range(ng):
        rows = slice(i * SUBLANES, (i + 1) * SUBLANES)
        pr = jnp.broadcast_to(sr_ref[last + i:last + i + 1, :], (SUBLANES, gp))
        pi = jnp.broadcast_to(si_ref[last + i:last + i + 1, :], (SUBLANES, gp))
        dr = l0r * pr - l0i * pi
        di = l0r * pi + l0i * pr
        for l in range(nl):
            xr_ref[l, rows, :] += tile(dr, l)
            xi_ref[l, rows, :] += tile(di, l)
    everything = slice(None)
    o_ref[...] = _ssm_tail(whole(xr_ref, everything), whole(xi_ref, everything), u,
                           wcr_ref, wci_ref, dsk_ref, wglu_ref)


def _ssm_prompt(u, prep, dsk, wglu_bf, b, t, lt=256):
    _, _, wcr, wci, lpr, lpi, amr, ami, wsr, wsi = prep
    m, sw = u.shape
    gp = lpr.shape[-1]
    nt = t // lt
    row = pl.BlockSpec((lt, sw), lambda bi, i: (bi * nt + i, 0))
    hspec = pl.BlockSpec((1, 1, gp), lambda bi, i: (bi, 0, 0))
    return pl.pallas_call(
        _ssm_scan_kernel,
        grid=(b, nt),
        in_specs=[row, _const_spec(wsr.shape), _const_spec(wsi.shape), _const_spec((gp, sw)),
                  _const_spec((gp, sw)), _const_spec(lpr.shape), _const_spec(lpi.shape),
                  _const_spec(amr.shape), _const_spec(ami.shape),
                  _const_spec((1, sw)), _const_spec((sw, sw))],
        out_specs=[row, hspec, hspec],
        out_shape=[jax.ShapeDtypeStruct((m, sw), F32), jax.ShapeDtypeStruct((b, 1, gp), F32),
                   jax.ShapeDtypeStruct((b, 1, gp), F32)],
        scratch_shapes=[pltpu.VMEM((gp // LANES, lt, LANES), F32)] * 2 + [pltpu.VMEM((SUBLANES, gp), F32)] * 2
                       + [pltpu.VMEM((SUBLANES + lt // SUBLANES, gp), F32)] * 2,
        compiler_params=_cparams(2),
        name="ssm_scan",
    )(u, wsr, wsi, wcr, wci, lpr, lpi, amr, ami, dsk, wglu_bf)


def _ssm_step_kernel(u_ref, h0r_ref, h0i_ref, wbr_ref, wbi_ref, wcr_ref, wci_ref, lpr_ref, lpi_ref,
                     dsk_ref, wglu_ref, o_ref, hr_ref, hi_ref):
    u = u_ref[...]
    ub = u.astype(BF16)
    lr = lpr_ref[0, 0:1, :]
    li = lpi_ref[0, 0:1, :]
    h0r = h0r_ref[...]
    h0i = h0i_ref[...]
    hr = _dot(ub, wbr_ref[...]) + (lr * h0r - li * h0i)
    hi = _dot(ub, wbi_ref[...]) + (lr * h0i + li * h0r)
    hr_ref[...] = hr
    hi_ref[...] = hi
    o_ref[...] = _ssm_tail(hr, hi, u, wcr_ref, wci_ref, dsk_ref, wglu_ref)


def _ssm_step(u, h0r, h0i, prep, dsk, wglu_bf):
    wbr, wbi, wcr, wci, lpr, lpi = prep[:6]
    m, sw = u.shape
    gp = wbr.shape[1]
    return pl.pallas_call(
        _ssm_step_kernel,
        out_shape=[jax.ShapeDtypeStruct((m, sw), F32), jax.ShapeDtypeStruct((m, gp), F32),
                   jax.ShapeDtypeStruct((m, gp), F32)],
        compiler_params=pltpu.CompilerParams(vmem_limit_bytes=VMEM_LIMIT),
        name="ssm_step",
    )(u, h0r, h0i, wbr, wbi, wcr, wci, lpr, lpi, dsk, wglu_bf)


def _mix_residual(x, att, ssm, ga_ref, gs_ref, wo_ref, gpm_ref):
    an = _rms(att, ga_ref[...]).astype(BF16)
    sn = _rms(ssm, gs_ref[...]).astype(BF16)
    aw = an.shape[-1]
    y = _dot(an, wo_ref[0:aw, :]) + _dot(sn, wo_ref[aw:, :])
    return x + _rms(y, gpm_ref[...])


def _gated_mlp(xn, wg_ref, wu_ref, wd_ref, conv, fc):
    f = wg_ref.shape[1]
    acc = None
    pending = None
    for cs in [slice(c * fc, (c + 1) * fc) for c in range(f // fc)]:
        gate = _dot(xn, wg_ref[:, cs])
        up = _dot(xn, wu_ref[:, cs])
        if pending is not None:
            part = _dot(pending[0], wd_ref[pending[1], :])
            acc = part if acc is None else acc + part
        pending = ((_gelu(conv(gate, cs)) * up).astype(BF16), cs)
    part = _dot(pending[0], wd_ref[pending[1], :])
    return part if acc is None else acc + part


def _tail_seq_kernel(x_ref, att_ref, ssm_ref, ga_ref, gs_ref, wo_ref, gpm_ref,
                     g_ref, wg_ref, wu_ref, cw_ref, cb_ref, wd_ref, gpost_ref,
                     o_ref, cn_ref, gate_ref, *, fc, blocks_per_seq):
    i = pl.program_id(0)
    bm = x_ref.shape[0]
    hist = SUBLANES
    first = (i % blocks_per_seq) == 0

    @pl.when(first)
    def _zero_hist():
        gate_ref[0:hist, :] = jnp.zeros((hist, gate_ref.shape[1]), F32)

    @pl.when(jnp.logical_not(first))
    def _keep_hist():
        gate_ref[0:hist, :] = gate_ref[bm:bm + hist, :]

    hb = bm // TAIL_SUB_BLOCKS
    x1s, xns = [], []
    for h in range(TAIL_SUB_BLOCKS):
        rows = slice(h * hb, (h + 1) * hb)
        x1 = _mix_residual(x_ref[rows, :], att_ref[rows, :], ssm_ref[rows, :], ga_ref, gs_ref, wo_ref, gpm_ref)
        x1s.append(x1)
        xns.append(_rms(x1, g_ref[...]).astype(BF16))
    for h in range(TAIL_SUB_BLOCKS):
        base = hist + h * hb

        def conv(gate, cs, base=base):
            gate_ref[base:base + hb, cs] = gate
            g1 = gate_ref[base - 1:base - 1 + hb, cs]
            g2 = gate_ref[base - 2:base - 2 + hb, cs]
            return cb_ref[:, cs] + g2 * cw_ref[0:1, cs] + g1 * cw_ref[1:2, cs] + gate * cw_ref[2:3, cs]

        acc = _gated_mlp(xns[h], wg_ref, wu_ref, wd_ref, conv, fc)
        o_ref[h * hb:(h + 1) * hb, :] = x1s[h] + _rms(acc, gpost_ref[...])
    cn_ref[0] = gate_ref[hist + bm - 2:hist + bm, :]


def _resident(shape):
    return pl.BlockSpec(shape, lambda *_: (0,) * len(shape), pipeline_mode=pl.Buffered(1))


def _tail_weight_specs(d, aw, sw, f):
    return [_const_spec((1, aw)), _const_spec((1, sw)), _resident((aw + sw, d)), _const_spec((1, d)),
            _const_spec((1, d)), _resident((d, f)), _resident((d, f)), _const_spec((3, f)),
            _const_spec((1, f)), _resident((f, d)), _const_spec((1, d))]


def _tail_seq(x, att, ssm, weights, b, t, bm=512, fc=256):
    m, d = x.shape
    aw, sw = att.shape[1], ssm.shape[1]
    f = weights[5].shape[1]
    bps = t // bm
    row = lambda w: pl.BlockSpec((bm, w), lambda i: (i, 0))
    return pl.pallas_call(
        functools.partial(_tail_seq_kernel, fc=fc, blocks_per_seq=bps),
        grid=(m // bm,),
        in_specs=[row(d), row(aw), row(sw)] + _tail_weight_specs(d, aw, sw, f),
        out_specs=[row(d), pl.BlockSpec((1, 2, f), lambda i: (i // bps, 0, 0))],
        out_shape=[jax.ShapeDtypeStruct((m, d), F32), jax.ShapeDtypeStruct((b, 2, f), F32)],
        scratch_shapes=[pltpu.VMEM((SUBLANES + bm, f), F32)],
        compiler_params=_cparams(1),
        name="tail_seq",
    )(x, att, ssm, *weights)


def _tail_step_kernel(x_ref, att_ref, ssm_ref, ga_ref, gs_ref, wo_ref, gpm_ref,
                      g_ref, wg_ref, wu_ref, cw_ref, cb_ref, wd_ref, gpost_ref, p0_ref, p1_ref,
                      o_ref, gate_ref, *, fc):
    x1 = _mix_residual(x_ref[...], att_ref[...], ssm_ref[...], ga_ref, gs_ref, wo_ref, gpm_ref)
    xn = _rms(x1, g_ref[...]).astype(BF16)

    def conv(gate, cs):
        gate_ref[:, cs] = gate
        return (cb_ref[:, cs] + p0_ref[:, cs] * cw_ref[0:1, cs] + p1_ref[:, cs] * cw_ref[1:2, cs]
                + gate * cw_ref[2:3, cs])

    acc = _gated_mlp(xn, wg_ref, wu_ref, wd_ref, conv, fc)
    o_ref[...] = x1 + _rms(acc, gpost_ref[...])


def _tail_step(x, att, ssm, weights, p0, p1, fc=256):
    m, d = x.shape
    f = weights[5].shape[1]
    return pl.pallas_call(
        functools.partial(_tail_step_kernel, fc=fc),
        out_shape=[jax.ShapeDtypeStruct((m, d), F32), jax.ShapeDtypeStruct((m, f), F32)],
        compiler_params=pltpu.CompilerParams(vmem_limit_bytes=VMEM_LIMIT),
        name="tail_step",
    )(x, att, ssm, *weights, p0, p1)


def kernel(x_prompt, x_sample, cache_k, cache_v, state_ssm_re, state_ssm_im, state_ffn_conv, page_table,
           g_pre_mix, w_in, sb_bias, g_att_out, lam_re, lam_im, log_dt, b_re, b_im, c_re, c_im, d_skip,
           w_glu, g_ssm_out, w_out, g_post_mix, g_pre_ffn, w_gate, w_up, conv_w, conv_b, w_down,
           g_post_ffn):
    depth = w_in.shape[0]
    assert depth == 1, "single-layer step"
    b, t, d = x_prompt.shape
    bs, ts, _ = x_sample.shape
    assert ts == 1, "sample group advances one token per sequence"
    n_heads, hd = cache_k.shape[-2:]
    assert hd == HEAD_DIM
    aw = n_heads * hd
    g, p = lam_re.shape[1:]
    l = 0

    row = lambda a: a[l].reshape(1, -1)
    w_in_bf = w_in[l].astype(BF16)
    wglu_bf = w_glu[l].astype(BF16)
    wo_bf = w_out[l].astype(BF16)
    wg_bf = w_gate[l].astype(BF16)
    wu_bf = w_up[l].astype(BF16)
    wd_bf = w_down[l].astype(BF16)
    bias = sb_bias[l]
    dsk = row(d_skip)
    tail_w = (row(g_att_out), row(g_ssm_out), wo_bf, row(g_post_mix), row(g_pre_ffn), wg_bf, wu_bf,
              conv_w[l], row(conv_b), wd_bf, row(g_post_ffn))
    prep = _ssm_prep(lam_re[l], lam_im[l], log_dt[l], b_re[l], b_im[l], c_re[l], c_im[l])

    xp = x_prompt.reshape(b * t, d)
    xs = x_sample.reshape(bs, d)
    qp, ktp, vtp, up = _inproj_seq(xp, row(g_pre_mix), w_in_bf, aw, b, t, bm=1024)
    qs, ks, vs, us = _inproj(xs, row(g_pre_mix), w_in_bf, aw, bm=bs)
    att_p, att_s = _attention(qp, ktp, vtp, qs, cache_k[l].transpose(0, 2, 3, 1),
                              cache_v[l].transpose(0, 2, 3, 1), page_table, bias, b, t)
    ssm_p, hrp, hip = _ssm_prompt(up, prep, dsk, wglu_bf, b, t)
    yp, conv_p = _tail_seq(xp, att_p, ssm_p, tail_w, b, t)

    ssm_s, hrs, his = _ssm_step(us, state_ssm_re[l].reshape(bs, g * p), state_ssm_im[l].reshape(bs, g * p),
                                prep, dsk, wglu_bf)
    prev = state_ffn_conv[l]
    ys, gate_s = _tail_step(xs, att_s, ssm_s, tail_w, prev[:, 0], prev[:, 1])
    conv_s = jnp.stack([prev[:, 1], gate_s], axis=1)

    heads = lambda a, n, s: a.reshape(1, n, s, n_heads, hd)
    heads_t = lambda a: a.reshape(1, b, n_heads, hd, t).transpose(0, 1, 4, 2, 3)
    state = lambda a, n: a.reshape(1, n, g, p)
    return (yp.reshape(b, t, d), ys.reshape(bs, 1, d),
            heads_t(ktp), heads_t(vtp), heads(ks, bs, 1), heads(vs, bs, 1),
            state(hrp, b), state(hip, b), state(hrs, bs), state(his, bs),
            conv_p[None], conv_s[None])
```
